```python
import jax
import jax.numpy as jnp
from jax import lax
import numpy as np

D_MODEL = 1024
BATCH = 8
SEQ = 4096
DEPTH = 2
DEC_BATCH = 8
DEC_SEQ = 8192
PAST_LEN = 128

N_EVEN = (DEPTH + 1) // 2
N_ODD = DEPTH // 2
D_FF = 2816
MIX_A = D_MODEL // 2
GM_GROUPS = 4
GM_CH = MIX_A // GM_GROUPS
GM_CHUNK = 128
MIX_B = D_MODEL // 2
RG_BLOCKS = 8
RG_BS = MIX_B // RG_BLOCKS
CONV_W = 4
RG_C = 8.0
ATT_HEADS = 16
ATT_HD = D_MODEL // ATT_HEADS
ROT_DIM = ATT_HD // 4
ROPE_THETA = 500000.0
DIL_PATTERNS = ((128, 1), (512, 4), (2048, 16))
N_MEM = 256
XA_HEADS = 4
XA_HD = D_MODEL // XA_HEADS
EPS = 1e-6
NEG_INF = -1e30

kernel_name = 'hybrid_bidir_encoder_two_batch'

F32 = jnp.float32


def rms_norm(x, g):
    xf = x.astype(F32)
    y = xf * lax.rsqrt(jnp.mean(xf * xf, axis=-1, keepdims=True) + EPS)
    return (y * g.astype(F32)).astype(x.dtype)


def layer_norm(x, g, b):
    xf = x.astype(F32)
    mu = jnp.mean(xf, axis=-1, keepdims=True)
    var = jnp.mean(jnp.square(xf - mu), axis=-1, keepdims=True)
    y = (xf - mu) * lax.rsqrt(var + EPS)
    return (y * g.astype(F32) + b.astype(F32)).astype(x.dtype)


def swiglu(h, w_in, w_out):
    gate, up = jnp.split(h @ w_in, 2, axis=-1)
    return (jax.nn.silu(gate) * up) @ w_out


def rope_partial(x, pos):
    half = ROT_DIM // 2
    inv = jnp.power(jnp.float32(ROPE_THETA), -jnp.arange(half, dtype=F32) * (2.0 / ROT_DIM))
    ang = pos[:, None] * inv[None, :]
    cos = jnp.cos(ang)[None, :, None, :]
    sin = jnp.sin(ang)[None, :, None, :]
    xr = x[..., :ROT_DIM].astype(F32)
    x1, x2 = xr[..., :half], xr[..., half:]
    rot = jnp.concatenate([x1 * cos - x2 * sin, x2 * cos + x1 * sin], axis=-1).astype(x.dtype)
    return jnp.concatenate([rot, x[..., ROT_DIM:]], axis=-1)


def gmlp_spatial(z_u, z_v, ln_g, ln_b, w_s, b_s):
    u = jax.nn.gelu(z_u)
    v = layer_norm(jax.nn.gelu(z_v), ln_g, ln_b)
    B, S, _ = v.shape
    nc = S // GM_CHUNK
    vc = v.reshape(B, nc, GM_CHUNK, GM_GROUPS, GM_CH)
    s = jnp.einsum('gpq,bnqgc->bnpgc', w_s, vc) + b_s.T[:, :, None]
    return u * s.reshape(B, S, MIX_A)


def centered_dwconv(x, w, b):
    S = x.shape[1]
    lpad = CONV_W // 2
    xp = jnp.pad(x, ((0, 0), (lpad, CONV_W - 1 - lpad), (0, 0)))
    y = sum(xp[:, k:k + S] * w[k] for k in range(CONV_W))
    return y + b


def _lin_combine(c1, c2):
    a1, b1 = c1
    a2, b2 = c2
    return a1 * a2, a2 * b1 + b2


def rg_lru_bidir(x, w_a, b_a, w_i, b_i, lam):
    B, S, _ = x.shape
    xb = x.reshape(B, S, RG_BLOCKS, RG_BS)
    total = jnp.zeros((B, S, MIX_B), F32)
    for d in range(2):
        r = jax.nn.sigmoid(jnp.einsum('bshi,hij->bshj', xb, w_a[d]).reshape(B, S, MIX_B) + b_a[d])
        i = jax.nn.sigmoid(jnp.einsum('bshi,hij->bshj', xb, w_i[d]).reshape(B, S, MIX_B) + b_i[d])
        log_a = (-RG_C * r.astype(F32)) * jax.nn.softplus(-lam[d].astype(F32))
        a = jnp.exp(log_a)
        u = x.astype(F32) * i.astype(F32) * jnp.sqrt(-jnp.expm1(2.0 * log_a))
        _, h = lax.associative_scan(_lin_combine, (a, u), reverse=(d == 1), axis=1)
        total = total + h
    return total.astype(x.dtype)


def even_mixer(h, w_in, w_out, ln_g, ln_b, w_s, b_s, conv_w, conv_b, w_a, b_a, w_i, b_i, lam):
    z = h @ w_in
    z_u, z_v, z_x, z_g = jnp.split(z, 4, axis=-1)
    a_out = gmlp_spatial(z_u, z_v, ln_g, ln_b, w_s, b_s)
    xc = centered_dwconv(z_x, conv_w, conv_b)
    b_out = rg_lru_bidir(xc, w_a, b_a, w_i, b_i, lam) * jax.nn.gelu(z_g)
    return jnp.concatenate([a_out, b_out], axis=-1) @ w_out


def dilated_window_attn(q, k, v, window, dil):
    B, S, H, E = q.shape
    half = window // (2 * dil)
    blk = half
    L = -(-S // (dil * blk)) * blk
    S_pad = L * dil
    nb = L // blk

    def to_blocks(t):
        t = jnp.pad(t, ((0, 0), (0, S_pad - S), (0, 0), (0, 0)))
        return t.reshape(B, nb, blk, dil, H, E)

    def neighbours(t):
        tb = jnp.pad(to_blocks(t), ((0, 0), (1, 1), (0, 0), (0, 0), (0, 0), (0, 0)))
        return jnp.concatenate([tb[:, :-2], tb[:, 1:-1], tb[:, 2:]], axis=2)

    qb = to_blocks(q)
    kn = neighbours(k)
    vn = neighbours(v)
    s_idx = jnp.arange(blk)
    t_idx = jnp.arange(3 * blk)
    band = jnp.abs((t_idx[None, :] - blk) - s_idx[:, None]) <= half
    m_k = (jnp.arange(nb)[:, None] - 1) * blk + t_idx[None, :]
    pos_k = m_k[:, :, None] * dil + jnp.arange(dil)[None, None, :]
    valid = (m_k[:, :, None] >= 0) & (pos_k < S)
    mask = band[None, None, :, :] & jnp.transpose(valid, (0, 2, 1))[:, :, None, :]

    scores = jnp.einsum('bjsrhe,bjtrhe->bjrhst', qb, kn).astype(F32)
    scores = jnp.where(mask[None, :, :, None, :, :], scores, NEG_INF)
    m = jnp.max(scores, axis=-1, keepdims=True)
    p = jnp.exp(scores - m)
    den = jnp.sum(p, axis=-1)
    den_t = jnp.transpose(den, (0, 1, 4, 2, 3))
    o = jnp.einsum('bjrhst,bjtrhe->bjsrhe', p.astype(v.dtype), vn).astype(F32) / den_t[..., None]
    lse = jnp.transpose(m[..., 0], (0, 1, 4, 2, 3)) + jnp.log(den_t)
    o = o.reshape(B, S_pad, H, E)[:, :S]
    lse = lse.reshape(B, S_pad, H)[:, :S]
    return o, lse


def odd_mixer(h, w_in, w_out, q_g, k_g):
    B, S, _ = h.shape
    q, k, v = jnp.split(h @ w_in, 3, axis=-1)
    q = q.reshape(B, S, ATT_HEADS, ATT_HD)
    k = k.reshape(B, S, ATT_HEADS, ATT_HD)
    v = v.reshape(B, S, ATT_HEADS, ATT_HD)
    pos = jnp.arange(S, dtype=F32)
    q = rope_partial(rms_norm(q, q_g), pos) * (ATT_HD ** -0.5)
    k = rope_partial(rms_norm(k, k_g), pos)
    outs = []
    lses = []
    for window, dil in DIL_PATTERNS:
        o_i, l_i = dilated_window_attn(q, k, v, window, dil)
        outs.append(o_i)
        lses.append(l_i)
    wts = jax.nn.softmax(jnp.stack(lses, axis=0), axis=0)
    o = sum(wts[i][..., None] * outs[i] for i in range(len(DIL_PATTERNS)))
    return o.astype(h.dtype).reshape(B, S, D_MODEL) @ w_out


def cross_attn(h, memn, w_q, w_kv, w_o, q_g, k_g):
    B, S, _ = h.shape
    M = memn.shape[1]
    q = rms_norm((h @ w_q).reshape(B, S, XA_HEADS, XA_HD), q_g)
    k, v = jnp.split(memn @ w_kv, 2, axis=-1)
    k = rms_norm(k.reshape(B, M, XA_HEADS, XA_HD), k_g)
    v = v.reshape(B, M, XA_HEADS, XA_HD)
    s = jnp.einsum('bshe,bmhe->bhsm', q, k).astype(F32) * (XA_HD ** -0.5)
    p = jax.nn.softmax(s, axis=-1).astype(v.dtype)
    o = jnp.einsum('bhsm,bmhe->bshe', p, v).reshape(B, S, D_MODEL)
    return o @ w_o


def trunk(x, mem, P):
    for l in range(DEPTH):
        x = x + 0.5 * swiglu(rms_norm(x, P['ffn1_norm'][l]), P['ffn1_w_in'][l], P['ffn1_w_out'][l])
        h = rms_norm(x, P['mix_norm'][l])
        e = l // 2
        if l % 2 == 0:
            x = x + even_mixer(h, P['ev_w_in'][e], P['ev_w_out'][e], P['gm_ln_g'][e], P['gm_ln_b'][e],
                               P['gm_w_s'][e], P['gm_b_s'][e], P['rg_conv_w'][e], P['rg_conv_b'][e],
                               P['rg_w_a'][e], P['rg_b_a'][e], P['rg_w_i'][e], P['rg_b_i'][e], P['rg_lam'][e])
        else:
            x = x + odd_mixer(h, P['od_w_in'][e], P['od_w_out'][e], P['od_q_norm'][e], P['od_k_norm'][e])
        x = x + cross_attn(rms_norm(x, P['xa_norm'][l]), rms_norm(mem, P['xa_mem_norm'][l]),
                           P['xa_w_q'][l], P['xa_w_kv'][l], P['xa_w_o'][l], P['xa_q_norm'][l], P['xa_k_norm'][l])
        x = x + 0.5 * swiglu(rms_norm(x, P['ffn2_norm'][l]), P['ffn2_w_in'][l], P['ffn2_w_out'][l])
    return x


def setup_inputs(seed: int = 0) -> dict:
    key = jax.random.key(seed)
    ks = iter(jax.random.split(key, 48))
    D = D_MODEL

    def w(shape, fan_in):
        return jax.random.normal(next(ks), shape, F32) * (fan_in ** -0.5)

    def gain(shape):
        return 1.0 + 0.02 * jax.random.normal(next(ks), shape, F32)

    def bias(shape):
        return 0.02 * jax.random.normal(next(ks), shape, F32)

    out = {}
    out['x_prompt'] = jax.random.normal(next(ks), (BATCH, SEQ, D), F32)
    out['x_sample'] = jax.random.normal(next(ks), (DEC_BATCH, DEC_SEQ, D), F32)
    out['mem_prompt'] = jax.random.normal(next(ks), (BATCH, N_MEM, D), F32)
    out['mem_sample'] = jax.random.normal(next(ks), (DEC_BATCH, N_MEM, D), F32)
    out['ffn1_norm'] = gain((DEPTH, D))
    out['ffn1_w_in'] = w((DEPTH, D, 2 * D_FF), D)
    out['ffn1_w_out'] = w((DEPTH, D_FF, D), D_FF)
    out['mix_norm'] = gain((DEPTH, D))
    out['ev_w_in'] = w((N_EVEN, D, 2 * MIX_A + 2 * MIX_B), D)
    out['ev_w_out'] = w((N_EVEN, MIX_A + MIX_B, D), MIX_A + MIX_B)
    out['gm_ln_g'] = gain((N_EVEN, MIX_A))
    out['gm_ln_b'] = bias((N_EVEN, MIX_A))
    out['gm_w_s'] = w((N_EVEN, GM_GROUPS, GM_CHUNK, GM_CHUNK), GM_CHUNK)
    out['gm_b_s'] = gain((N_EVEN, GM_GROUPS, GM_CHUNK))
    out['rg_conv_w'] = w((N_EVEN, CONV_W, MIX_B), CONV_W)
    out['rg_conv_b'] = bias((N_EVEN, MIX_B))
    out['rg_w_a'] = w((N_EVEN, 2, RG_BLOCKS, RG_BS, RG_BS), RG_BS)
    out['rg_b_a'] = bias((N_EVEN, 2, MIX_B))
    out['rg_w_i'] = w((N_EVEN, 2, RG_BLOCKS, RG_BS, RG_BS), RG_BS)
    out['rg_b_i'] = bias((N_EVEN, 2, MIX_B))
    lam_u = jax.random.uniform(next(ks), (N_EVEN, 2, MIX_B), F32, 0.9, 0.999)
    base = lam_u ** (1.0 / RG_C)
    out['rg_lam'] = jnp.log(base) - jnp.log1p(-base)
    out['od_w_in'] = w((N_ODD, D, 3 * D), D)
    out['od_w_out'] = w((N_ODD, D, D), D)
    out['od_q_norm'] = gain((N_ODD, ATT_HD))
    out['od_k_norm'] = gain((N_ODD, ATT_HD))
    out['xa_norm'] = gain((DEPTH, D))
    out['xa_mem_norm'] = gain((DEPTH, D))
    out['xa_w_q'] = w((DEPTH, D, D), D)
    out['xa_w_kv'] = w((DEPTH, D, 2 * D), D)
    out['xa_w_o'] = w((DEPTH, D, D), D)
    out['xa_q_norm'] = gain((DEPTH, XA_HD))
    out['xa_k_norm'] = gain((DEPTH, XA_HD))
    out['ffn2_norm'] = gain((DEPTH, D))
    out['ffn2_w_in'] = w((DEPTH, D, 2 * D_FF), D)
    out['ffn2_w_out'] = w((DEPTH, D_FF, D), D_FF)
    return out


def reference(x_prompt, x_sample, mem_prompt, mem_sample,
              ffn1_norm, ffn1_w_in, ffn1_w_out, mix_norm,
              ev_w_in, ev_w_out, gm_ln_g, gm_ln_b, gm_w_s, gm_b_s,
              rg_conv_w, rg_conv_b, rg_w_a, rg_b_a, rg_w_i, rg_b_i, rg_lam,
              od_w_in, od_w_out, od_q_norm, od_k_norm,
              xa_norm, xa_mem_norm, xa_w_q, xa_w_kv, xa_w_o, xa_q_norm, xa_k_norm,
              ffn2_norm, ffn2_w_in, ffn2_w_out):
    P = dict(ffn1_norm=ffn1_norm, ffn1_w_in=ffn1_w_in, ffn1_w_out=ffn1_w_out, mix_norm=mix_norm,
             ev_w_in=ev_w_in, ev_w_out=ev_w_out, gm_ln_g=gm_ln_g, gm_ln_b=gm_ln_b,
             gm_w_s=gm_w_s, gm_b_s=gm_b_s, rg_conv_w=rg_conv_w, rg_conv_b=rg_conv_b,
             rg_w_a=rg_w_a, rg_b_a=rg_b_a, rg_w_i=rg_w_i, rg_b_i=rg_b_i, rg_lam=rg_lam,
             od_w_in=od_w_in, od_w_out=od_w_out, od_q_norm=od_q_norm, od_k_norm=od_k_norm,
             xa_norm=xa_norm, xa_mem_norm=xa_mem_norm, xa_w_q=xa_w_q, xa_w_kv=xa_w_kv,
             xa_w_o=xa_w_o, xa_q_norm=xa_q_norm, xa_k_norm=xa_k_norm,
             ffn2_norm=ffn2_norm, ffn2_w_in=ffn2_w_in, ffn2_w_out=ffn2_w_out)
    y_prompt = trunk(x_prompt, mem_prompt, P)
    y_sample = trunk(x_sample, mem_sample, P)
    return (y_prompt, y_sample)
```

```python
import functools

import numpy as np
import jax
import jax.numpy as jnp
from jax import lax
from jax.experimental import pallas as pl
from jax.experimental.pallas import tpu as pltpu

F32 = jnp.float32
BF16 = jnp.bfloat16

EPS = 1e-6
NEG_INF = -1e30
RG_C = 8.0
GM_CHUNK = 128
GM_GROUPS = 4
CONV_W = 4
ATT_HD = 64
ROT_DIM = 16
ROPE_THETA = 500000.0
DIL_PATTERNS = ((128, 1), (512, 4), (2048, 16))
ATT_HALF = 64
XA_HD = 256

LANES = 128
MXU_N = 256
HALO = 16
Q_TILE = 128
K_WIN = Q_TILE + 2 * ATT_HALF
SUPER = Q_TILE * 16

TM_FFN = 512
TM_MIX = 256
FF_CHUNK = 256
VMEM_LIMIT = 56 * 1024 * 1024


def _params(n_axes):
    return pltpu.CompilerParams(dimension_semantics=("arbitrary",) * n_axes,
                                vmem_limit_bytes=VMEM_LIMIT)


def _full(a):
    nd = a.ndim
    return pl.BlockSpec(a.shape, lambda *_: (0,) * nd)


def _dot(a, b):
    return jnp.dot(a, b, preferred_element_type=F32)


def _dot_nt(a, b):
    return lax.dot_general(a, b, (((1,), (1,)), ((), ())), preferred_element_type=F32)


def _rms(x, g):
    return x * lax.rsqrt(jnp.mean(x * x, axis=-1, keepdims=True) + EPS) * g


def _gelu(x):
    c = np.float32(np.sqrt(2.0 / np.pi))
    return x * (0.5 * (1.0 + jnp.tanh(c * (x + 0.044715 * (x * x * x)))))


def _sigmoid(x):
    return 1.0 / (1.0 + jnp.exp(-x))


def _ffn_kernel(x_ref, g_ref, win_ref, wout_ref, o_ref, act_ref, *, d_ff):
    x = x_ref[...]
    h = _rms(x, g_ref[...]).astype(BF16)
    for c in range(d_ff // FF_CHUNK):
        lo = c * FF_CHUNK
        gate = _dot(h, win_ref[:, lo:lo + FF_CHUNK])
        up = _dot(h, win_ref[:, d_ff + lo:d_ff + lo + FF_CHUNK])
        act_ref[:, lo:lo + FF_CHUNK] = (gate * _sigmoid(gate) * up).astype(BF16)
    o_ref[...] = x + 0.5 * _dot(act_ref[...], wout_ref[...])


def _ffn(x, g, w_in, w_out):
    B, S, D = x.shape
    d_ff = w_out.shape[0]
    tm = TM_FFN
    row = pl.BlockSpec((None, tm, D), lambda b, t: (b, t, 0))
    return pl.pallas_call(
        functools.partial(_ffn_kernel, d_ff=d_ff),
        grid=(B, S // tm),
        in_specs=[row, _full(g), _full(w_in), _full(w_out)],
        out_specs=row,
        out_shape=jax.ShapeDtypeStruct(x.shape, F32),
        scratch_shapes=[pltpu.VMEM((tm, d_ff), BF16)],
        compiler_params=_params(2),
        name="ffn",
    )(x, g, w_in, w_out)


def _conv_input(x_ref, xp_ref, xn_ref, g_ref, wx, cw_ref, cb_ref, ext_ref, tile, n_tiles, tm):
    g = g_ref[...]
    zx = _dot(_rms(x_ref[...], g).astype(BF16), wx)
    zp = _dot(_rms(xp_ref[...], g).astype(BF16), wx)
    zn = _dot(_rms(xn_ref[...], g).astype(BF16), wx)
    ext_ref[0:HALO, :] = jnp.where(tile > 0, zp, 0.0)
    ext_ref[HALO:HALO + tm, :] = zx
    ext_ref[HALO + tm:2 * HALO + tm, :] = jnp.where(tile < n_tiles - 1, zn, 0.0)
    lpad = CONV_W // 2
    acc = None
    for k in range(CONV_W):
        term = ext_ref[pl.ds(HALO - lpad + k, tm), :] * cw_ref[k:k + 1, :]
        acc = term if acc is None else acc + term
    return acc + cb_ref[...]


def _lru_inputs(xc, gw_ref, ba_ref, bi_ref, lam_ref):
    xcb = xc.astype(BF16)
    half = xcb.shape[1] // 2

    def gate(which, bias_ref):
        lo = _dot(xcb[:, :half], gw_ref[which, 0])
        hi = _dot(xcb[:, half:], gw_ref[which, 1])
        return _sigmoid(jnp.concatenate([lo, hi], axis=1) + bias_ref[...])

    r = gate(0, ba_ref)
    i = gate(1, bi_ref)
    z = -lam_ref[...]
    softplus = jnp.maximum(z, 0.0) + jnp.log1p(jnp.exp(-jnp.abs(z)))
    log_a = (-RG_C * r) * softplus
    a = jnp.exp(log_a)
    t = jnp.tanh(log_a)
    u = xc * i * jnp.sqrt(-2.0 * t / (1.0 - t))
    return a, u


def _group_scan(a, u, reverse):
    tm, c = a.shape
    a3 = a.reshape(tm // 8, 8, c)
    u3 = u.reshape(tm // 8, 8, c)
    row = lax.broadcasted_iota(jnp.int32, a3.shape, 1)
    for k in (1, 2, 4):
        shift, valid = (8 - k, row < 8 - k) if reverse else (k, row >= k)
        a_sh = jnp.where(valid, pltpu.roll(a3, shift, 1), 1.0)
        u_sh = jnp.where(valid, pltpu.roll(u3, shift, 1), 0.0)
        u3 = u3 + a3 * u_sh
        a3 = a3 * a_sh
    return a3.reshape(tm, c), u3.reshape(tm, c)


def _carry_scan(a_ref, l_ref, h_ref, carry_ref, tm, reverse):
    ng = tm // 8

    def body(j, c):
        g = ng - 1 - j if reverse else j
        r0 = pl.multiple_of(g * 8, 8)
        h = l_ref[pl.ds(r0, 8), :] + a_ref[pl.ds(r0, 8), :] * c
        h_ref[pl.ds(r0, 8), :] = h
        return h[0:1, :] if reverse else h[7:8, :]

    carry_ref[0:1, :] = lax.fori_loop(0, ng, body, carry_ref[0:1, :], unroll=4)


def _ev_bwd_kernel(x_ref, xp_ref, xn_ref, g_ref, wx_ref, cw_ref, cb_ref, gw_ref, ba_ref, bi_ref,
                   lam_ref, h_ref, ext_ref, a_ref, l_ref, carry_ref, *, tm, n_tiles):
    step = pl.program_id(1)
    tile = n_tiles - 1 - step

    @pl.when(step == 0)
    def _():
        carry_ref[...] = jnp.zeros_like(carry_ref)

    xc = _conv_input(x_ref, xp_ref, xn_ref, g_ref, wx_ref[...], cw_ref, cb_ref, ext_ref,
                     tile, n_tiles, tm)
    a, u = _lru_inputs(xc, gw_ref, ba_ref, bi_ref, lam_ref)
    a_ref[...], l_ref[...] = _group_scan(a, u, reverse=True)
    _carry_scan(a_ref, l_ref, h_ref, carry_ref, tm, reverse=True)


def _ev_main_kernel(x_ref, xp_ref, xn_ref, hb_ref, g_ref, win_ref, cw_ref, cb_ref, gw_ref, ba_ref,
                    bi_ref, lam_ref, lng_ref, lnb_ref, ws_ref, bs_ref, wout_ref, o_ref,
                    ext_ref, a_ref, l_ref, h_ref, carry_ref, cat_ref, *, tm, n_tiles, mix):
    tile = pl.program_id(1)

    @pl.when(tile == 0)
    def _():
        carry_ref[...] = jnp.zeros_like(carry_ref)

    x = x_ref[...]
    xc = _conv_input(x_ref, xp_ref, xn_ref, g_ref, win_ref[:, 2 * mix:3 * mix], cw_ref, cb_ref,
                     ext_ref, tile, n_tiles, tm)
    a, u = _lru_inputs(xc, gw_ref, ba_ref, bi_ref, lam_ref)
    a_ref[...], l_ref[...] = _group_scan(a, u, reverse=False)
    _carry_scan(a_ref, l_ref, h_ref, carry_ref, tm, reverse=False)

    h = _rms(x, g_ref[...]).astype(BF16)
    z_g = _dot(h, win_ref[:, 3 * mix:4 * mix])
    cat_ref[:, mix:2 * mix] = ((h_ref[...] + hb_ref[...]) * _gelu(z_g)).astype(BF16)

    u_act = _gelu(_dot(h, win_ref[:, 0:mix]))
    v_act = _gelu(_dot(h, win_ref[:, mix:2 * mix]))
    mu = jnp.mean(v_act, axis=-1, keepdims=True)
    var = jnp.mean(jnp.square(v_act - mu), axis=-1, keepdims=True)
    v = ((v_act - mu) * lax.rsqrt(var + EPS) * lng_ref[...] + lnb_ref[...]).astype(BF16)
    gch = mix // GM_GROUPS
    for c in range(tm // GM_CHUNK):
        rows = slice(c * GM_CHUNK, (c + 1) * GM_CHUNK)
        for gi in range(GM_GROUPS):
            cols = slice(gi * gch, (gi + 1) * gch)
            s = _dot(ws_ref[gi], v[rows, cols]) + bs_ref[gi]
            cat_ref[rows, cols] = (u_act[rows, cols] * s).astype(BF16)

    o_ref[...] = x + _dot(cat_ref[...], wout_ref[...])


def _block_diag_tiles(w):
    nb, bs, _ = w.shape
    per = MXU_N // bs
    tiles = []
    for h in range(nb // per):
        t = jnp.zeros((MXU_N, MXU_N), w.dtype)
        for j in range(per):
            t = lax.dynamic_update_slice(t, w[h * per + j], (j * bs, j * bs))
        tiles.append(t)
    return jnp.stack(tiles)


def _even_mixer(x, g, w_in, w_out, ln_g, ln_b, w_s, b_s, conv_w, conv_b, w_a, b_a, w_i, b_i, lam):
    B, S, D = x.shape
    mix = w_in.shape[1] // 4
    tm = TM_MIX
    n_tiles = S // tm
    hpt = tm // HALO
    n_halo = S // HALO

    gw = [jnp.stack([_block_diag_tiles(w_a[d]), _block_diag_tiles(w_i[d])]).astype(BF16)
          for d in range(2)]
    row2 = lambda v: v.reshape(1, -1)
    bs_b = jnp.broadcast_to(b_s[:, :, None], (GM_GROUPS, GM_CHUNK, mix // GM_GROUPS))

    def specs(tile_of):
        return [
            pl.BlockSpec((None, tm, D), lambda b, t: (b, tile_of(t), 0)),
            pl.BlockSpec((None, HALO, D), lambda b, t: (b, jnp.maximum(tile_of(t) * hpt - 1, 0), 0)),
            pl.BlockSpec((None, HALO, D),
                         lambda b, t: (b, jnp.minimum((tile_of(t) + 1) * hpt, n_halo - 1), 0)),
        ]

    rev = lambda t: n_tiles - 1 - t
    wx = w_in[:, 2 * mix:3 * mix]
    bwd_args = (g, wx, conv_w, row2(conv_b), gw[1], row2(b_a[1]), row2(b_i[1]), row2(lam[1]))
    h_bwd = pl.pallas_call(
        functools.partial(_ev_bwd_kernel, tm=tm, n_tiles=n_tiles),
        grid=(B, n_tiles),
        in_specs=specs(rev) + [_full(a) for a in bwd_args],
        out_specs=pl.BlockSpec((None, tm, mix), lambda b, t: (b, rev(t), 0)),
        out_shape=jax.ShapeDtypeStruct((B, S, mix), F32),
        scratch_shapes=[pltpu.VMEM((tm + 2 * HALO, mix), F32), pltpu.VMEM((tm, mix), F32),
                        pltpu.VMEM((tm, mix), F32), pltpu.VMEM((8, mix), F32)],
        compiler_params=_params(2),
        name="ev_bwd",
    )(x, x, x, *bwd_args)

    fwd = lambda t: t
    main_args = (g, w_in, conv_w, row2(conv_b), gw[0], row2(b_a[0]), row2(b_i[0]), row2(lam[0]),
                 row2(ln_g), row2(ln_b), w_s, bs_b, w_out)
    return pl.pallas_call(
        functools.partial(_ev_main_kernel, tm=tm, n_tiles=n_tiles, mix=mix),
        grid=(B, n_tiles),
        in_specs=specs(fwd) + [pl.BlockSpec((None, tm, mix), lambda b, t: (b, t, 0))]
        + [_full(a) for a in main_args],
        out_specs=pl.BlockSpec((None, tm, D), lambda b, t: (b, t, 0)),
        out_shape=jax.ShapeDtypeStruct(x.shape, F32),
        scratch_shapes=[pltpu.VMEM((tm + 2 * HALO, mix), F32), pltpu.VMEM((tm, mix), F32),
                        pltpu.VMEM((tm, mix), F32), pltpu.VMEM((tm, mix), F32),
                        pltpu.VMEM((8, mix), F32), pltpu.VMEM((tm, 2 * mix), BF16)],
        compiler_params=_params(2),
        name="ev_main",
    )(x, x, x, h_bwd, *main_args)


def _od_proj_kernel(x_ref, g_ref, w_ref, gsum_ref, qg_ref, kg_ref, cos_ref, sa_ref, sb_ref,
                    q_ref, k_ref, v_ref, *, d):
    h = _rms(x_ref[...], g_ref[...]).astype(BF16)
    cos, sa, sb = cos_ref[...], sa_ref[...], sb_ref[...]
    half = ROT_DIM // 2

    def normed_rotated(col0, gain_ref, out_ref, scale):
        y = _dot(h, w_ref[:, col0:col0 + d])
        sq = (y * y).astype(BF16)
        ss = jnp.concatenate([_dot(sq[:, j:j + MXU_N], gsum_ref[...]) for j in range(0, d, MXU_N)],
                             axis=1)
        yn = y * lax.rsqrt(ss * (1.0 / ATT_HD) + EPS) * gain_ref[...]
        for hp in range(d // LANES):
            blk = yn[:, hp * LANES:(hp + 1) * LANES]
            rot = (blk * cos + pltpu.roll(blk, half, 1) * sa
                   + pltpu.roll(blk, LANES - half, 1) * sb)
            out_ref[hp] = (rot * scale).astype(BF16)

    normed_rotated(0, qg_ref, q_ref, ATT_HD ** -0.5)
    normed_rotated(d, kg_ref, k_ref, 1.0)
    v = _dot(h, w_ref[:, 2 * d:3 * d]).astype(BF16)
    for hp in range(d // LANES):
        v_ref[hp] = v[:, hp * LANES:(hp + 1) * LANES]


def _rope_tables(S):
    half = ROT_DIM // 2
    inv = jnp.power(jnp.float32(ROPE_THETA), -jnp.arange(half, dtype=F32) * (2.0 / ROT_DIM))
    ang = jnp.arange(S, dtype=F32)[:, None] * inv[None, :]
    lane = np.arange(LANES) % ATT_HD
    idx = lane % half
    cos = jnp.where(lane < ROT_DIM, jnp.cos(ang)[:, idx], 1.0)
    sin = jnp.sin(ang)[:, idx]
    sin_a = jnp.where((lane >= half) & (lane < ROT_DIM), sin, 0.0)
    sin_b = jnp.where(lane < half, -sin, 0.0)
    return cos, sin_a, sin_b


def _attn_tile(q, k, v, off, old, dji, head0):
    dd = dji - off
    bias = jnp.where((dd >= -ATT_HALF) & (dd <= ATT_HALF), 0.0, NEG_INF)
    zero = jnp.zeros_like(q)
    s0 = _dot_nt(jnp.where(head0, q, zero), k) + bias
    s1 = _dot_nt(jnp.where(head0, zero, q), k) + bias
    m_new = jnp.where(head0, jnp.max(s0, axis=1, keepdims=True), jnp.max(s1, axis=1, keepdims=True))
    if old is not None:
        m_old, l_old, acc_old = old
        m_new = jnp.maximum(m_old, m_new)
    p0 = jnp.exp(s0 - m_new[:, 0:1])
    p1 = jnp.exp(s1 - m_new[:, ATT_HD:ATT_HD + 1])
    l_new = jnp.where(head0, jnp.sum(p0, axis=1, keepdims=True), jnp.sum(p1, axis=1, keepdims=True))
    acc_new = jnp.where(head0, _dot(p0.astype(BF16), v), _dot(p1.astype(BF16), v))
    if old is not None:
        alpha = jnp.exp(m_old - m_new)
        l_new = alpha * l_old + l_new
        acc_new = alpha * acc_old + acc_new
    return m_new, l_new, acc_new


def _od_attn_kernel(q1_ref, k1_ref, v1_ref, q4_ref, k4_ref, v4_ref, q16_ref, k16_ref, v16_ref,
                    o_ref, m_ref, l_ref, acc_ref, *, S):
    head0 = lax.broadcasted_iota(jnp.int32, (1, LANES), 1) < ATT_HD
    dji = (lax.broadcasted_iota(jnp.int32, (Q_TILE, K_WIN), 1)
           - lax.broadcasted_iota(jnp.int32, (Q_TILE, K_WIN), 0))

    def window(q0, n):
        k0 = pl.multiple_of(jnp.clip(q0 - ATT_HALF, 0, n - K_WIN), ATT_HALF)
        return k0, q0 - k0

    def super_tile(st, _):
        q0 = pl.multiple_of(st * Q_TILE, Q_TILE)
        k0, off = window(q0, S // 16)
        for r in range(16):
            cols = slice(r * LANES, (r + 1) * LANES)
            m, l, acc = _attn_tile(q16_ref[pl.ds(q0, Q_TILE), cols], k16_ref[pl.ds(k0, K_WIN), cols],
                                   v16_ref[pl.ds(k0, K_WIN), cols], off, None, dji, head0)
            rows = pl.ds(r, Q_TILE, stride=16)
            m_ref[rows, :] = m
            l_ref[rows, :] = l
            acc_ref[rows, :] = acc

        for r in range(4):
            cols = slice(r * LANES, (r + 1) * LANES)

            def tile4(tq, _):
                q0 = pl.multiple_of(st * (SUPER // 4) + tq * Q_TILE, Q_TILE)
                k0, off = window(q0, S // 4)
                rows = pl.ds(tq * (4 * Q_TILE) + r, Q_TILE, stride=4)
                m, l, acc = _attn_tile(q4_ref[pl.ds(q0, Q_TILE), cols], k4_ref[pl.ds(k0, K_WIN), cols],
                                       v4_ref[pl.ds(k0, K_WIN), cols], off,
                                       (m_ref[rows, :], l_ref[rows, :], acc_ref[rows, :]), dji, head0)
                m_ref[rows, :] = m
                l_ref[rows, :] = l
                acc_ref[rows, :] = acc
                return 0

            lax.fori_loop(0, 4, tile4, 0)

        def tile1(tq, _):
            r0 = pl.multiple_of(tq * Q_TILE, Q_TILE)
            q0 = pl.multiple_of(st * SUPER + r0, Q_TILE)
            k0, off = window(q0, S)
            rows = pl.ds(r0, Q_TILE)
            _, l, acc = _attn_tile(q1_ref[pl.ds(q0, Q_TILE), :], k1_ref[pl.ds(k0, K_WIN), :],
                                   v1_ref[pl.ds(k0, K_WIN), :], off,
                                   (m_ref[rows, :], l_ref[rows, :], acc_ref[rows, :]), dji, head0)
            o_ref[pl.ds(q0, Q_TILE), :] = (acc / l).astype(BF16)
            return 0

        lax.fori_loop(0, 16, tile1, 0)
        return 0

    lax.fori_loop(0, S // SUPER, super_tile, 0)


def _proj_res_kernel(x_ref, o_ref_in, w_ref, out_ref):
    out_ref[...] = x_ref[...] + _dot(o_ref_in[...], w_ref[...])


def _odd_mixer(x, g, w_in, w_out, q_g, k_g):
    B, S, D = x.shape
    tm = TM_MIX
    n_hp = D // LANES
    n_heads = D // ATT_HD
    assert S % SUPER == 0 and S // 16 >= K_WIN

    blk = jnp.kron(jnp.eye(MXU_N // ATT_HD, dtype=F32), jnp.ones((ATT_HD, ATT_HD), F32)).astype(BF16)
    tiled = lambda v: jnp.tile(v, n_heads).reshape(1, D)
    tables = _rope_tables(S)
    tab_spec = pl.BlockSpec((tm, LANES), lambda b, t: (t, 0))
    hp_spec = pl.BlockSpec((None, n_hp, tm, LANES), lambda b, t: (b, 0, t, 0))
    hp_shape = jax.ShapeDtypeStruct((B, n_hp, S, LANES), BF16)
    args = (g, w_in, blk, tiled(q_g), tiled(k_g))
    q, k, v = pl.pallas_call(
        functools.partial(_od_proj_kernel, d=D),
        grid=(B, S // tm),
        in_specs=[pl.BlockSpec((None, tm, D), lambda b, t: (b, t, 0))] + [_full(a) for a in args]
        + [tab_spec] * 3,
        out_specs=[hp_spec] * 3,
        out_shape=[hp_shape] * 3,
        compiler_params=_params(2),
        name="od_proj",
    )(x, *args, *tables)

    views, view_specs = [], []
    for _, dil in DIL_PATTERNS:
        shape = (B, n_hp, S // dil, dil * LANES)
        views += [t.reshape(shape) for t in (q, k, v)]
        view_specs += [pl.BlockSpec((None, None) + shape[2:], lambda b, h: (b, h, 0, 0))] * 3
    o = pl.pallas_call(
        functools.partial(_od_attn_kernel, S=S),
        grid=(B, n_hp),
        in_specs=view_specs,
        out_specs=pl.BlockSpec((None, S, LANES), lambda b, h: (b, 0, h)),
        out_shape=jax.ShapeDtypeStruct((B, S, D), BF16),
        scratch_shapes=[pltpu.VMEM((SUPER, LANES), F32)] * 3,
        compiler_params=_params(2),
        name="od_attn",
    )(*views)

    row = lambda dt: pl.BlockSpec((None, tm, D), lambda b, t: (b, t, 0))
    return pl.pallas_call(
        _proj_res_kernel,
        grid=(B, S // tm),
        in_specs=[row(F32), row(BF16), _full(w_out)],
        out_specs=row(F32),
        out_shape=jax.ShapeDtypeStruct(x.shape, F32),
        compiler_params=_params(2),
        name="od_out",
    )(x, o, w_out)


def _head_rms(y, ones, gain):
    sq = (y * y).astype(BF16)
    ss = jnp.concatenate([_dot(sq[:, j:j + XA_HD], ones) for j in range(0, y.shape[1], XA_HD)], axis=1)
    return y * lax.rsqrt(ss * (1.0 / XA_HD) + EPS) * gain


def _xa_kv_kernel(mem_ref, g_ref, w_ref, ones_ref, kg_ref, k_ref, v_ref, *, d):
    h = _rms(mem_ref[...], g_ref[...]).astype(BF16)
    k_ref[...] = _head_rms(_dot(h, w_ref[:, 0:d]), ones_ref[...], kg_ref[...]).astype(BF16)
    v_ref[...] = _dot(h, w_ref[:, d:2 * d]).astype(BF16)


def _xa_main_kernel(x_ref, g_ref, wq_ref, ones_ref, qg_ref, k_ref, v_ref, wo_ref, o_ref, cat_ref):
    x = x_ref[...]
    h = _rms(x, g_ref[...]).astype(BF16)
    q = _head_rms(_dot(h, wq_ref[...]), ones_ref[...], qg_ref[...]).astype(BF16)
    for j in range(0, x.shape[1], XA_HD):
        s = _dot_nt(q[:, j:j + XA_HD], k_ref[:, j:j + XA_HD]) * (XA_HD ** -0.5)
        p = jnp.exp(s - jnp.max(s, axis=1, keepdims=True))
        p = (p / jnp.sum(p, axis=1, keepdims=True)).astype(BF16)
        cat_ref[:, j:j + XA_HD] = _dot(p, v_ref[:, j:j + XA_HD]).astype(BF16)
    o_ref[...] = x + _dot(cat_ref[...], wo_ref[...])


def _cross_attn(x, mem, g_x, g_mem, w_q, w_kv, w_o, q_g, k_g):
    B, S, D = x.shape
    M = mem.shape[1]
    tm = TM_MIX
    n_heads = D // XA_HD
    ones = jnp.ones((XA_HD, XA_HD), BF16)
    tiled = lambda v: jnp.tile(v, n_heads).reshape(1, D)
    mem_spec = lambda dt: pl.BlockSpec((None, M, D), lambda b, *_: (b, 0, 0))
    kv_args = (g_mem, w_kv, ones, tiled(k_g))
    k, v = pl.pallas_call(
        functools.partial(_xa_kv_kernel, d=D),
        grid=(B,),
        in_specs=[mem_spec(F32)] + [_full(a) for a in kv_args],
        out_specs=[mem_spec(BF16)] * 2,
        out_shape=[jax.ShapeDtypeStruct((B, M, D), BF16)] * 2,
        compiler_params=_params(1),
        name="xa_kv",
    )(mem, *kv_args)

    row = pl.BlockSpec((None, tm, D), lambda b, t: (b, t, 0))
    q_args = (g_x, w_q, ones, tiled(q_g))
    return pl.pallas_call(
        _xa_main_kernel,
        grid=(B, S // tm),
        in_specs=[row] + [_full(a) for a in q_args] + [mem_spec(BF16)] * 2 + [_full(w_o)],
        out_specs=row,
        out_shape=jax.ShapeDtypeStruct(x.shape, F32),
        scratch_shapes=[pltpu.VMEM((tm, D), BF16)],
        compiler_params=_params(2),
        name="xa_main",
    )(x, *q_args, k, v, w_o)


def _trunk(x, mem, P):
    depth = P['ffn1_norm'].shape[0]
    row = lambda v: v.reshape(1, -1)
    for l in range(depth):
        x = _ffn(x, row(P['ffn1_norm'][l]), P['ffn1_w_in'][l], P['ffn1_w_out'][l])
        g = row(P['mix_norm'][l])
        e = l // 2
        if l % 2 == 0:
            x = _even_mixer(x, g, P['ev_w_in'][e], P['ev_w_out'][e], P['gm_ln_g'][e], P['gm_ln_b'][e],
                            P['gm_w_s'][e], P['gm_b_s'][e], P['rg_conv_w'][e], P['rg_conv_b'][e],
                            P['rg_w_a'][e], P['rg_b_a'][e], P['rg_w_i'][e], P['rg_b_i'][e],
                            P['rg_lam'][e])
        else:
            x = _odd_mixer(x, g, P['od_w_in'][e], P['od_w_out'][e], P['od_q_norm'][e],
                           P['od_k_norm'][e])
        x = _cross_attn(x, mem, row(P['xa_norm'][l]), row(P['xa_mem_norm'][l]), P['xa_w_q'][l],
                        P['xa_w_kv'][l], P['xa_w_o'][l], P['xa_q_norm'][l], P['xa_k_norm'][l])
        x = _ffn(x, row(P['ffn2_norm'][l]), P['ffn2_w_in'][l], P['ffn2_w_out'][l])
    return x


_MXU_WEIGHTS = ('ffn1_w_in', 'ffn1_w_out', 'ev_w_in', 'ev_w_out', 'gm_w_s', 'od_w_in', 'od_w_out',
                'xa_w_q', 'xa_w_kv', 'xa_w_o', 'ffn2_w_in', 'ffn2_w_out')


def kernel(x_prompt, x_sample, mem_prompt, mem_sample, ffn1_norm, ffn1_w_in, ffn1_w_out, mix_norm, ev_w_in, ev_w_out, gm_ln_g, gm_ln_b, gm_w_s, gm_b_s, rg_conv_w, rg_conv_b, rg_w_a, rg_b_a, rg_w_i, rg_b_i, rg_lam, od_w_in, od_w_out, od_q_norm, od_k_norm, xa_norm, xa_mem_norm, xa_w_q, xa_w_kv, xa_w_o, xa_q_norm, xa_k_norm, ffn2_norm, ffn2_w_in, ffn2_w_out):
    P = dict(ffn1_norm=ffn1_norm, ffn1_w_in=ffn1_w_in, ffn1_w_out=ffn1_w_out, mix_norm=mix_norm,
             ev_w_in=ev_w_in, ev_w_out=ev_w_out, gm_ln_g=gm_ln_g, gm_ln_b=gm_ln_b,
             gm_w_s=gm_w_s, gm_b_s=gm_b_s, rg_conv_w=rg_conv_w, rg_conv_b=rg_conv_b,
             rg_w_a=rg_w_a, rg_b_a=rg_b_a, rg_w_i=rg_w_i, rg_b_i=rg_b_i, rg_lam=rg_lam,
             od_w_in=od_w_in, od_w_out=od_w_out, od_q_norm=od_q_norm, od_k_norm=od_k_norm,
             xa_norm=xa_norm, xa_mem_norm=xa_mem_norm, xa_w_q=xa_w_q, xa_w_kv=xa_w_kv,
             xa_w_o=xa_w_o, xa_q_norm=xa_q_norm, xa_k_norm=xa_k_norm,
             ffn2_norm=ffn2_norm, ffn2_w_in=ffn2_w_in, ffn2_w_out=ffn2_w_out)
    for name in _MXU_WEIGHTS:
        P[name] = P[name].astype(BF16)
    return (_trunk(x_prompt, mem_prompt, P), _trunk(x_sample, mem_sample, P))
```

```python
import functools

import numpy as np
import jax
import jax.numpy as jnp
from jax import lax
from jax.experimental import pallas as pl
from jax.experimental.pallas import tpu as pltpu

F32 = jnp.float32
BF16 = jnp.bfloat16

EPS = 1e-6
NEG_INF = -1e30
RG_C = 8.0
GM_CHUNK = 128
GM_GROUPS = 4
CONV_W = 4
ATT_HD = 64
ROT_DIM = 16
ROPE_THETA = 500000.0
DIL_PATTERNS = ((128, 1), (512, 4), (2048, 16))
ATT_HALF = 64
XA_HD = 256

LANES = 128
MXU_N = 256
HALO = 16
Q_TILE = 128
K_WIN = Q_TILE + 2 * ATT_HALF
SUPER = Q_TILE * 16

TM_FFN = 512
TM_MIX = 256
FF_CHUNK = 256
VMEM_LIMIT = 56 * 1024 * 1024


def _params(n_axes):
    return pltpu.CompilerParams(dimension_semantics=("arbitrary",) * n_axes,
                                vmem_limit_bytes=VMEM_LIMIT)


def _full(a):
    nd = a.ndim
    return pl.BlockSpec(a.shape, lambda *_: (0,) * nd)


def _dot(a, b):
    return jnp.dot(a, b, preferred_element_type=F32)


def _dot_nt(a, b):
    return lax.dot_general(a, b, (((1,), (1,)), ((), ())), preferred_element_type=F32)


def _rms(x, g):
    return x * lax.rsqrt(jnp.mean(x * x, axis=-1, keepdims=True) + EPS) * g


def _gelu(x):
    c = np.float32(np.sqrt(2.0 / np.pi))
    return x * (0.5 * (1.0 + jnp.tanh(c * (x + 0.044715 * (x * x * x)))))


def _sigmoid(x):
    return 1.0 / (1.0 + jnp.exp(-x))


def _ffn_kernel(x_ref, g_ref, win_ref, wout_ref, o_ref, act_ref, *, d_ff):
    x = x_ref[...]
    h = _rms(x, g_ref[...]).astype(BF16)
    for c in range(d_ff // FF_CHUNK):
        lo = c * FF_CHUNK
        gate = _dot(h, win_ref[:, lo:lo + FF_CHUNK])
        up = _dot(h, win_ref[:, d_ff + lo:d_ff + lo + FF_CHUNK])
        act_ref[:, lo:lo + FF_CHUNK] = (gate * _sigmoid(gate) * up).astype(BF16)
    o_ref[...] = x + 0.5 * _dot(act_ref[...], wout_ref[...])


def _ffn(x, g, w_in, w_out):
    B, S, D = x.shape
    d_ff = w_out.shape[0]
    tm = TM_FFN
    row = pl.BlockSpec((None, tm, D), lambda b, t: (b, t, 0))
    return pl.pallas_call(
        functools.partial(_ffn_kernel, d_ff=d_ff),
        grid=(B, S // tm),
        in_specs=[row, _full(g), _full(w_in), _full(w_out)],
        out_specs=row,
        out_shape=jax.ShapeDtypeStruct(x.shape, F32),
        scratch_shapes=[pltpu.VMEM((tm, d_ff), BF16)],
        compiler_params=_params(2),
        name="ffn",
    )(x, g, w_in, w_out)


def _conv_input(x_ref, xp_ref, xn_ref, g_ref, wx, cw_ref, cb_ref, ext_ref, tile, n_tiles, tm):
    g = g_ref[...]
    zx = _dot(_rms(x_ref[...], g).astype(BF16), wx)
    zp = _dot(_rms(xp_ref[...], g).astype(BF16), wx)
    zn = _dot(_rms(xn_ref[...], g).astype(BF16), wx)
    ext_ref[0:HALO, :] = jnp.where(tile > 0, zp, 0.0)
    ext_ref[HALO:HALO + tm, :] = zx
    ext_ref[HALO + tm:2 * HALO + tm, :] = jnp.where(tile < n_tiles - 1, zn, 0.0)
    lpad = CONV_W // 2
    acc = None
    for k in range(CONV_W):
        term = ext_ref[pl.ds(HALO - lpad + k, tm), :] * cw_ref[k:k + 1, :]
        acc = term if acc is None else acc + term
    return acc + cb_ref[...]


def _lru_inputs(xc, gw_ref, ba_ref, bi_ref, lam_ref):
    xcb = xc.astype(BF16)
    half = xcb.shape[1] // 2

    def gate(which, bias_ref):
        lo = _dot(xcb[:, :half], gw_ref[which, 0])
        hi = _dot(xcb[:, half:], gw_ref[which, 1])
        return _sigmoid(jnp.concatenate([lo, hi], axis=1) + bias_ref[...])

    r = gate(0, ba_ref)
    i = gate(1, bi_ref)
    z = -lam_ref[...]
    softplus = jnp.maximum(z, 0.0) + jnp.log1p(jnp.exp(-jnp.abs(z)))
    log_a = (-RG_C * r) * softplus
    a = jnp.exp(log_a)
    t = jnp.tanh(log_a)
    u = xc * i * jnp.sqrt(-2.0 * t / (1.0 - t))
    return a, u


def _group_scan(a, u, reverse):
    tm, c = a.shape
    a3 = a.reshape(tm // 8, 8, c)
    u3 = u.reshape(tm // 8, 8, c)
    row = lax.broadcasted_iota(jnp.int32, a3.shape, 1)
    for k in (1, 2, 4):
        shift, valid = (8 - k, row < 8 - k) if reverse else (k, row >= k)
        a_sh = jnp.where(valid, pltpu.roll(a3, shift, 1), 1.0)
        u_sh = jnp.where(valid, pltpu.roll(u3, shift, 1), 0.0)
        u3 = u3 + a3 * u_sh
        a3 = a3 * a_sh
    return a3.reshape(tm, c), u3.reshape(tm, c)


def _carry_scan(a_ref, l_ref, h_ref, carry_ref, tm, reverse):
    ng = tm // 8

    def body(j, c):
        g = ng - 1 - j if reverse else j
        r0 = pl.multiple_of(g * 8, 8)
        h = l_ref[pl.ds(r0, 8), :] + a_ref[pl.ds(r0, 8), :] * c
        h_ref[pl.ds(r0, 8), :] = h
        return h[0:1, :] if reverse else h[7:8, :]

    carry_ref[0:1, :] = lax.fori_loop(0, ng, body, carry_ref[0:1, :], unroll=4)


def _ev_bwd_kernel(x_ref, xp_ref, xn_ref, g_ref, wx_ref, cw_ref, cb_ref, gw_ref, ba_ref, bi_ref,
                   lam_ref, h_ref, ext_ref, a_ref, l_ref, carry_ref, *, tm, n_tiles):
    step = pl.program_id(1)
    tile = n_tiles - 1 - step

    @pl.when(step == 0)
    def _():
        carry_ref[...] = jnp.zeros_like(carry_ref)

    xc = _conv_input(x_ref, xp_ref, xn_ref, g_ref, wx_ref[...], cw_ref, cb_ref, ext_ref,
                     tile, n_tiles, tm)
    a, u = _lru_inputs(xc, gw_ref, ba_ref, bi_ref, lam_ref)
    a_ref[...], l_ref[...] = _group_scan(a, u, reverse=True)
    _carry_scan(a_ref, l_ref, h_ref, carry_ref, tm, reverse=True)


def _ev_main_kernel(x_ref, xp_ref, xn_ref, hb_ref, g_ref, win_ref, cw_ref, cb_ref, gw_ref, ba_ref,
                    bi_ref, lam_ref, lng_ref, lnb_ref, ws_ref, bs_ref, wout_ref, o_ref,
                    ext_ref, a_ref, l_ref, h_ref, carry_ref, cat_ref, *, tm, n_tiles, mix):
    tile = pl.program_id(1)

    @pl.when(tile == 0)
    def _():
        carry_ref[...] = jnp.zeros_like(carry_ref)

    x = x_ref[...]
    xc = _conv_input(x_ref, xp_ref, xn_ref, g_ref, win_ref[:, 2 * mix:3 * mix], cw_ref, cb_ref,
                     ext_ref, tile, n_tiles, tm)
    a, u = _lru_inputs(xc, gw_ref, ba_ref, bi_ref, lam_ref)
    a_ref[...], l_ref[...] = _group_scan(a, u, reverse=False)
    _carry_scan(a_ref, l_ref, h_ref, carry_ref, tm, reverse=False)

    h = _rms(x, g_ref[...]).astype(BF16)
    z_g = _dot(h, win_ref[:, 3 * mix:4 * mix])
    cat_ref[:, mix:2 * mix] = ((h_ref[...] + hb_ref[...]) * _gelu(z_g)).astype(BF16)

    u_act = _gelu(_dot(h, win_ref[:, 0:mix]))
    v_act = _gelu(_dot(h, win_ref[:, mix:2 * mix]))
    mu = jnp.mean(v_act, axis=-1, keepdims=True)
    var = jnp.mean(jnp.square(v_act - mu), axis=-1, keepdims=True)
    v = ((v_act - mu) * lax.rsqrt(var + EPS) * lng_ref[...] + lnb_ref[...]).astype(BF16)
    gch = mix // GM_GROUPS
    for c in range(tm // GM_CHUNK):
        rows = slice(c * GM_CHUNK, (c + 1) * GM_CHUNK)
        for gi in range(GM_GROUPS):
            cols = slice(gi * gch, (gi + 1) * gch)
            s = _dot(ws_ref[gi], v[rows, cols]) + bs_ref[gi]
            cat_ref[rows, cols] = (u_act[rows, cols] * s).astype(BF16)

    o_ref[...] = x + _dot(cat_ref[...], wout_ref[...])


def _block_diag_tiles(w):
    nb, bs, _ = w.shape
    per = MXU_N // bs
    tiles = []
    for h in range(nb // per):
        t = jnp.zeros((MXU_N, MXU_N), w.dtype)
        for j in range(per):
            t = lax.dynamic_update_slice(t, w[h * per + j], (j * bs, j * bs))
        tiles.append(t)
    return jnp.stack(tiles)


def _even_mixer(x, g, w_in, w_out, ln_g, ln_b, w_s, b_s, conv_w, conv_b, w_a, b_a, w_i, b_i, lam):
    B, S, D = x.shape
    mix = w_in.shape[1] // 4
    tm = TM_MIX
    n_tiles = S // tm
    hpt = tm // HALO
    n_halo = S // HALO

    gw = [jnp.stack([_block_diag_tiles(w_a[d]), _block_diag_tiles(w_i[d])]).astype(BF16)
          for d in range(2)]
    row2 = lambda v: v.reshape(1, -1)
    bs_b = jnp.broadcast_to(b_s[:, :, None], (GM_GROUPS, GM_CHUNK, mix // GM_GROUPS))

    def specs(tile_of):
        return [
            pl.BlockSpec((None, tm, D), lambda b, t: (b, tile_of(t), 0)),
            pl.BlockSpec((None, HALO, D), lambda b, t: (b, jnp.maximum(tile_of(t) * hpt - 1, 0), 0)),
            pl.BlockSpec((None, HALO, D),
                         lambda b, t: (b, jnp.minimum((tile_of(t) + 1) * hpt, n_halo - 1), 0)),
        ]

    rev = lambda t: n_tiles - 1 - t
    wx = w_in[:, 2 * mix:3 * mix]
    bwd_args = (g, wx, conv_w, row2(conv_b), gw[1], row2(b_a[1]), row2(b_i[1]), row2(lam[1]))
    h_bwd = pl.pallas_call(
        functools.partial(_ev_bwd_kernel, tm=tm, n_tiles=n_tiles),
        grid=(B, n_tiles),
        in_specs=specs(rev) + [_full(a) for a in bwd_args],
        out_specs=pl.BlockSpec((None, tm, mix), lambda b, t: (b, rev(t), 0)),
        out_shape=jax.ShapeDtypeStruct((B, S, mix), F32),
        scratch_shapes=[pltpu.VMEM((tm + 2 * HALO, mix), F32), pltpu.VMEM((tm, mix), F32),
                        pltpu.VMEM((tm, mix), F32), pltpu.VMEM((8, mix), F32)],
        compiler_params=_params(2),
        name="ev_bwd",
    )(x, x, x, *bwd_args)

    fwd = lambda t: t
    main_args = (g, w_in, conv_w, row2(conv_b), gw[0], row2(b_a[0]), row2(b_i[0]), row2(lam[0]),
                 row2(ln_g), row2(ln_b), w_s, bs_b, w_out)
    return pl.pallas_call(
        functools.partial(_ev_main_kernel, tm=tm, n_tiles=n_tiles, mix=mix),
        grid=(B, n_tiles),
        in_specs=specs(fwd) + [pl.BlockSpec((None, tm, mix), lambda b, t: (b, t, 0))]
        + [_full(a) for a in main_args],
        out_specs=pl.BlockSpec((None, tm, D), lambda b, t: (b, t, 0)),
        out_shape=jax.ShapeDtypeStruct(x.shape, F32),
        scratch_shapes=[pltpu.VMEM((tm + 2 * HALO, mix), F32), pltpu.VMEM((tm, mix), F32),
                        pltpu.VMEM((tm, mix), F32), pltpu.VMEM((tm, mix), F32),
                        pltpu.VMEM((8, mix), F32), pltpu.VMEM((tm, 2 * mix), BF16)],
        compiler_params=_params(2),
        name="ev_main",
    )(x, x, x, h_bwd, *main_args)


def _od_proj_kernel(x_ref, g_ref, w_ref, gsum_ref, qg_ref, kg_ref, cos_ref, sa_ref, sb_ref,
                    q1_ref, k1_ref, v1_ref, q4_ref, k4_ref, v4_ref, q16_ref, k16_ref, v16_ref,
                    nat_ref, d4_ref, *, d, tm):
    h = _rms(x_ref[...], g_ref[...]).astype(BF16)
    cos, sa, sb = cos_ref[...], sa_ref[...], sb_ref[...]
    half = ROT_DIM // 2

    def emit(hp, blk, o1_ref, o4_ref, o16_ref):
        o1_ref[hp] = blk.astype(BF16)
        nat_ref[hp] = blk
        for r1 in range(4):
            a = nat_ref[hp, pl.ds(r1, tm // 4, stride=4), :]
            o4_ref[hp, :, r1 * LANES:(r1 + 1) * LANES] = a.astype(BF16)
            d4_ref[4 * hp + r1] = a
            for r2 in range(4):
                b = d4_ref[4 * hp + r1, pl.ds(r2, tm // 16, stride=4), :]
                r = 4 * r2 + r1
                o16_ref[hp, :, r * LANES:(r + 1) * LANES] = b.astype(BF16)

    def normed_rotated(col0, gain_ref, outs, scale):
        y = _dot(h, w_ref[:, col0:col0 + d])
        sq = (y * y).astype(BF16)
        ss = jnp.concatenate([_dot(sq[:, j:j + MXU_N], gsum_ref[...]) for j in range(0, d, MXU_N)],
                             axis=1)
        yn = y * lax.rsqrt(ss * (1.0 / ATT_HD) + EPS) * gain_ref[...]
        for hp in range(d // LANES):
            blk = yn[:, hp * LANES:(hp + 1) * LANES]
            rot = (blk * cos + pltpu.roll(blk, half, 1) * sa
                   + pltpu.roll(blk, LANES - half, 1) * sb)
            emit(hp, rot * scale, *outs)

    normed_rotated(0, qg_ref, (q1_ref, q4_ref, q16_ref), ATT_HD ** -0.5)
    normed_rotated(d, kg_ref, (k1_ref, k4_ref, k16_ref), 1.0)
    v = _dot(h, w_ref[:, 2 * d:3 * d])
    for hp in range(d // LANES):
        emit(hp, v[:, hp * LANES:(hp + 1) * LANES], v1_ref, v4_ref, v16_ref)


def _rope_tables(S):
    half = ROT_DIM // 2
    inv = jnp.power(jnp.float32(ROPE_THETA), -jnp.arange(half, dtype=F32) * (2.0 / ROT_DIM))
    ang = jnp.arange(S, dtype=F32)[:, None] * inv[None, :]
    lane = np.arange(LANES) % ATT_HD
    idx = lane % half
    cos = jnp.where(lane < ROT_DIM, jnp.cos(ang)[:, idx], 1.0)
    sin = jnp.sin(ang)[:, idx]
    sin_a = jnp.where((lane >= half) & (lane < ROT_DIM), sin, 0.0)
    sin_b = jnp.where(lane < half, -sin, 0.0)
    return cos, sin_a, sin_b


def _keep_lanes(x, keep):
    return pltpu.bitcast(pltpu.bitcast(x, jnp.uint32) & keep, BF16)


def _attn_scores(q, k, bias, keep0, keep1):
    return (_dot_nt(_keep_lanes(q, keep0), k) + bias, _dot_nt(_keep_lanes(q, keep1), k) + bias)


def _attn_max(s0, s1, old):
    m0 = jnp.max(s0, axis=1, keepdims=True)
    m1 = jnp.max(s1, axis=1, keepdims=True)
    if old is None:
        return m0, m1
    return jnp.maximum(old[0], m0), jnp.maximum(old[1], m1)


def _sub_rows(s, m):
    if m.shape[1] == 1:
        return s - m
    return jnp.concatenate([s[:, j:j + LANES] - m for j in range(0, s.shape[1], LANES)], axis=1)


def _attn_update(s0, s1, v, m0, m1, old, head0, keep0, keep1, ones0, ones1):
    p = jnp.concatenate([jnp.exp(_sub_rows(s0, m0)), jnp.exp(_sub_rows(s1, m1))], axis=1).astype(BF16)
    rhs = jnp.concatenate([jnp.concatenate([_keep_lanes(v, keep0), ones0], axis=1),
                           jnp.concatenate([_keep_lanes(v, keep1), ones1], axis=1)], axis=0)
    r = _dot(p, rhs)
    acc, l = r[:, :LANES], r[:, LANES:]
    if old is not None:
        alpha = jnp.where(head0, jnp.exp(old[0] - m0), jnp.exp(old[1] - m1))
        l = alpha * old[2] + l
        acc = alpha * old[3] + acc
    full = (Q_TILE, LANES)
    return jnp.broadcast_to(m0, full), jnp.broadcast_to(m1, full), l, acc


def _od_attn_kernel(bias_ref, q1_ref, k1_ref, v1_ref, q4_ref, k4_ref, v4_ref, q16_ref, k16_ref,
                    v16_ref, o_ref, m0_ref, m1_ref, l_ref, acc_ref, *, S):
    head0 = lax.broadcasted_iota(jnp.int32, (1, LANES), 1) < ATT_HD
    all_bits = jnp.uint32(0xFFFFFFFF)
    keep0 = jnp.where(head0, all_bits, jnp.uint32(0))
    keep1 = jnp.where(head0, jnp.uint32(0), all_bits)
    ones0 = jnp.broadcast_to(jnp.where(head0, 1.0, 0.0).astype(BF16), (K_WIN, LANES))
    ones1 = jnp.broadcast_to(jnp.where(head0, 0.0, 1.0).astype(BF16), (K_WIN, LANES))
    state_refs = (m0_ref, m1_ref, l_ref, acc_ref)

    def window(q0, n):
        k0 = pl.multiple_of(jnp.clip(q0 - ATT_HALF, 0, n - K_WIN), ATT_HALF)
        return k0, lax.div(q0 - k0, ATT_HALF)

    def read_state(rows):
        return tuple(ref[rows, :] for ref in state_refs)

    def super_tile(st, _):
        tasks = []
        for r in range(16):
            tasks.append(((q16_ref, k16_ref, v16_ref), st * Q_TILE, S // 16,
                          slice(r * LANES, (r + 1) * LANES), pl.ds(r, Q_TILE, stride=16), False, False))
        for r in range(4):
            for tq in range(4):
                tasks.append(((q4_ref, k4_ref, v4_ref), st * (SUPER // 4) + tq * Q_TILE, S // 4,
                              slice(r * LANES, (r + 1) * LANES),
                              pl.ds(tq * (4 * Q_TILE) + r, Q_TILE, stride=4), True, False))
        for tq in range(16):
            tasks.append(((q1_ref, k1_ref, v1_ref), st * SUPER + tq * Q_TILE, S, slice(None),
                          pl.ds(tq * Q_TILE, Q_TILE), True, True))

        def stage_scores(task):
            (q_ref, k_ref, v_ref), q0, n, cols, _, _, _ = task
            q0 = pl.multiple_of(q0, Q_TILE)
            k0, bidx = window(q0, n)
            s0, s1 = _attn_scores(q_ref[pl.ds(q0, Q_TILE), cols], k_ref[pl.ds(k0, K_WIN), cols],
                                  bias_ref[bidx], keep0, keep1)
            return s0, s1, v_ref[pl.ds(k0, K_WIN), cols], q0

        def stage_max(task, carry):
            s0, s1, v, q0 = carry
            old = read_state(task[4]) if task[5] else None
            return s0, s1, v, q0, _attn_max(s0, s1, old), old

        def stage_update(task, carry):
            s0, s1, v, q0, (m0, m1), old = carry
            state = _attn_update(s0, s1, v, m0, m1, old, head0, keep0, keep1, ones0, ones1)
            if task[6]:
                _, _, l, acc = state
                o_ref[pl.ds(q0, Q_TILE), :] = (acc / l).astype(BF16)
            else:
                for ref, val in zip(state_refs, state):
                    ref[task[4], :] = val

        n = len(tasks)
        after_scores, after_max = {}, {}
        for t in range(n + 2):
            if t < n:
                after_scores[t] = stage_scores(tasks[t])
            if 0 <= t - 1 < n:
                after_max[t - 1] = stage_max(tasks[t - 1], after_scores.pop(t - 1))
            if 0 <= t - 2 < n:
                stage_update(tasks[t - 2], after_max.pop(t - 2))
        return 0

    lax.fori_loop(0, S // SUPER, super_tile, 0)


def _band_bias():
    dji = np.arange(K_WIN)[None, :] - np.arange(Q_TILE)[:, None]
    offs = np.arange(3)[:, None, None] * ATT_HALF
    return np.where(np.abs(dji[None] - offs) <= ATT_HALF, 0.0, NEG_INF).astype(np.float32)


def _odd_mixer(x, g, w_in, q_g, k_g):
    B, S, D = x.shape
    tm = TM_MIX
    n_hp = D // LANES
    n_heads = D // ATT_HD
    assert S % SUPER == 0 and S // 16 >= K_WIN and tm % 256 == 0

    blk = jnp.kron(jnp.eye(MXU_N // ATT_HD, dtype=F32), jnp.ones((ATT_HD, ATT_HD), F32)).astype(BF16)
    tiled = lambda v: jnp.tile(v, n_heads).reshape(1, D)
    tables = _rope_tables(S)
    tab_spec = pl.BlockSpec((tm, LANES), lambda b, t: (t, 0))
    view_out_specs, view_shapes, view_in_specs = [], [], []
    for _, dil in DIL_PATTERNS:
        rows, cols = S // dil, dil * LANES
        view_out_specs += [pl.BlockSpec((None, n_hp, tm // dil, cols), lambda b, t: (b, 0, t, 0))] * 3
        view_shapes += [jax.ShapeDtypeStruct((B, n_hp, rows, cols), BF16)] * 3
        view_in_specs += [pl.BlockSpec((None, None, rows, cols), lambda b, h: (b, h, 0, 0))] * 3
    args = (g, w_in, blk, tiled(q_g), tiled(k_g))
    views = pl.pallas_call(
        functools.partial(_od_proj_kernel, d=D, tm=tm),
        grid=(B, S // tm),
        in_specs=[pl.BlockSpec((None, tm, D), lambda b, t: (b, t, 0))] + [_full(a) for a in args]
        + [tab_spec] * 3,
        out_specs=view_out_specs,
        out_shape=view_shapes,
        scratch_shapes=[pltpu.VMEM((n_hp, tm, LANES), F32), pltpu.VMEM((4 * n_hp, tm // 4, LANES), F32)],
        compiler_params=_params(2),
        name="od_proj",
    )(x, *args, *tables)

    bias = jnp.asarray(_band_bias())
    return pl.pallas_call(
        functools.partial(_od_attn_kernel, S=S),
        grid=(B, n_hp),
        in_specs=[_full(bias)] + view_in_specs,
        out_specs=pl.BlockSpec((None, S, LANES), lambda b, h: (b, 0, h)),
        out_shape=jax.ShapeDtypeStruct((B, S, D), BF16),
        scratch_shapes=[pltpu.VMEM((SUPER, LANES), F32)] * 4,
        compiler_params=_params(2),
        name="od_attn",
    )(bias, *views)


def _head_rms(y, ones, gain):
    sq = (y * y).astype(BF16)
    ss = jnp.concatenate([_dot(sq[:, j:j + XA_HD], ones) for j in range(0, y.shape[1], XA_HD)], axis=1)
    return y * lax.rsqrt(ss * (1.0 / XA_HD) + EPS) * gain


def _xa_kv_kernel(mem_ref, g_ref, w_ref, ones_ref, kg_ref, k_ref, v_ref, *, d):
    h = _rms(mem_ref[...], g_ref[...]).astype(BF16)
    k_ref[...] = _head_rms(_dot(h, w_ref[:, 0:d]), ones_ref[...], kg_ref[...]).astype(BF16)
    v_ref[...] = _dot(h, w_ref[:, d:2 * d]).astype(BF16)


def _xa_main_kernel(*refs, pre_proj):
    if pre_proj:
        pre_ref, wpre_ref, *refs = refs
    x_ref, g_ref, wq_ref, ones_ref, qg_ref, k_ref, v_ref, wo_ref, o_ref, cat_ref = refs
    x = x_ref[...]
    if pre_proj:
        x = x + _dot(pre_ref[...], wpre_ref[...])
    h = _rms(x, g_ref[...]).astype(BF16)
    q = _head_rms(_dot(h, wq_ref[...]), ones_ref[...], qg_ref[...]).astype(BF16)
    for j in range(0, x.shape[1], XA_HD):
        s = _dot_nt(q[:, j:j + XA_HD], k_ref[:, j:j + XA_HD]) * (XA_HD ** -0.5)
        p = jnp.exp(s - jnp.max(s, axis=1, keepdims=True))
        p = (p / jnp.sum(p, axis=1, keepdims=True)).astype(BF16)
        cat_ref[:, j:j + XA_HD] = _dot(p, v_ref[:, j:j + XA_HD]).astype(BF16)
    o_ref[...] = x + _dot(cat_ref[...], wo_ref[...])


def _cross_attn(x, mem, g_x, g_mem, w_q, w_kv, w_o, q_g, k_g, pre=None):
    B, S, D = x.shape
    M = mem.shape[1]
    tm = TM_MIX
    n_heads = D // XA_HD
    ones = jnp.ones((XA_HD, XA_HD), BF16)
    tiled = lambda v: jnp.tile(v, n_heads).reshape(1, D)
    mem_spec = lambda dt: pl.BlockSpec((None, M, D), lambda b, *_: (b, 0, 0))
    kv_args = (g_mem, w_kv, ones, tiled(k_g))
    k, v = pl.pallas_call(
        functools.partial(_xa_kv_kernel, d=D),
        grid=(B,),
        in_specs=[mem_spec(F32)] + [_full(a) for a in kv_args],
        out_specs=[mem_spec(BF16)] * 2,
        out_shape=[jax.ShapeDtypeStruct((B, M, D), BF16)] * 2,
        compiler_params=_params(1),
        name="xa_kv",
    )(mem, *kv_args)

    row = pl.BlockSpec((None, tm, D), lambda b, t: (b, t, 0))
    q_args = (g_x, w_q, ones, tiled(q_g))
    pre_args = () if pre is None else pre
    pre_specs = [] if pre is None else [row, _full(pre[1])]
    return pl.pallas_call(
        functools.partial(_xa_main_kernel, pre_proj=pre is not None),
        grid=(B, S // tm),
        in_specs=pre_specs + [row] + [_full(a) for a in q_args] + [mem_spec(BF16)] * 2 + [_full(w_o)],
        out_specs=row,
        out_shape=jax.ShapeDtypeStruct(x.shape, F32),
        scratch_shapes=[pltpu.VMEM((tm, D), BF16)],
        compiler_params=_params(2),
        name="xa_main",
    )(*pre_args, x, *q_args, k, v, w_o)


def _trunk(x, mem, P):
    depth = P['ffn1_norm'].shape[0]
    row = lambda v: v.reshape(1, -1)
    for l in range(depth):
        x = _ffn(x, row(P['ffn1_norm'][l]), P['ffn1_w_in'][l], P['ffn1_w_out'][l])
        g = row(P['mix_norm'][l])
        e = l // 2
        pre = None
        if l % 2 == 0:
            x = _even_mixer(x, g, P['ev_w_in'][e], P['ev_w_out'][e], P['gm_ln_g'][e], P['gm_ln_b'][e],
                            P['gm_w_s'][e], P['gm_b_s'][e], P['rg_conv_w'][e], P['rg_conv_b'][e],
                            P['rg_w_a'][e], P['rg_b_a'][e], P['rg_w_i'][e], P['rg_b_i'][e],
                            P['rg_lam'][e])
        else:
            pre = (_odd_mixer(x, g, P['od_w_in'][e], P['od_q_norm'][e], P['od_k_norm'][e]),
                   P['od_w_out'][e])
        x = _cross_attn(x, mem, row(P['xa_norm'][l]), row(P['xa_mem_norm'][l]), P['xa_w_q'][l],
                        P['xa_w_kv'][l], P['xa_w_o'][l], P['xa_q_norm'][l], P['xa_k_norm'][l], pre)
        x = _ffn(x, row(P['ffn2_norm'][l]), P['ffn2_w_in'][l], P['ffn2_w_out'][l])
    return x


_MXU_WEIGHTS = ('ffn1_w_in', 'ffn1_w_out', 'ev_w_in', 'ev_w_out', 'gm_w_s', 'od_w_in', 'od_w_out',
                'xa_w_q', 'xa_w_kv', 'xa_w_o', 'ffn2_w_in', 'ffn2_w_out')


def kernel(x_prompt, x_sample, mem_prompt, mem_sample, ffn1_norm, ffn1_w_in, ffn1_w_out, mix_norm, ev_w_in, ev_w_out, gm_ln_g, gm_ln_b, gm_w_s, gm_b_s, rg_conv_w, rg_conv_b, rg_w_a, rg_b_a, rg_w_i, rg_b_i, rg_lam, od_w_in, od_w_out, od_q_norm, od_k_norm, xa_norm, xa_mem_norm, xa_w_q, xa_w_kv, xa_w_o, xa_q_norm, xa_k_norm, ffn2_norm, ffn2_w_in, ffn2_w_out):
    P = dict(ffn1_norm=ffn1_norm, ffn1_w_in=ffn1_w_in, ffn1_w_out=ffn1_w_out, mix_norm=mix_norm,
             ev_w_in=ev_w_in, ev_w_out=ev_w_out, gm_ln_g=gm_ln_g, gm_ln_b=gm_ln_b,
             gm_w_s=gm_w_s, gm_b_s=gm_b_s, rg_conv_w=rg_conv_w, rg_conv_b=rg_conv_b,
             rg_w_a=rg_w_a, rg_b_a=rg_b_a, rg_w_i=rg_w_i, rg_b_i=rg_b_i, rg_lam=rg_lam,
             od_w_in=od_w_in, od_w_out=od_w_out, od_q_norm=od_q_norm, od_k_norm=od_k_norm,
             xa_norm=xa_norm, xa_mem_norm=xa_mem_norm, xa_w_q=xa_w_q, xa_w_kv=xa_w_kv,
             xa_w_o=xa_w_o, xa_q_norm=xa_q_norm, xa_k_norm=xa_k_norm,
             ffn2_norm=ffn2_norm, ffn2_w_in=ffn2_w_in, ffn2_w_out=ffn2_w_out)
    for name in _MXU_WEIGHTS:
        P[name] = P[name].astype(BF16)
    return (_trunk(x_prompt, mem_prompt, P), _trunk(x_sample, mem_sample, P))
```

```python
import functools

import numpy as np
import jax
import jax.numpy as jnp
from jax import lax
from jax.experimental import pallas as pl
from jax.experimental.pallas import tpu as pltpu

F32 = jnp.float32
BF16 = jnp.bfloat16

EPS = 1e-6
NEG_INF = -1e30
RG_C = 8.0
GM_CHUNK = 128
GM_GROUPS = 4
CONV_W = 4
ATT_HD = 64
ROT_DIM = 16
ROPE_THETA = 500000.0
DIL_PATTERNS = ((128, 1), (512, 4), (2048, 16))
ATT_HALF = 64
XA_HD = 256

LANES = 128
MXU_N = 256
HALO = 16
Q_TILE = 128
K_WIN = Q_TILE + 2 * ATT_HALF
SUPER = Q_TILE * 16

TM_FFN = 512
TM_MIX = 512
TM_XA = 512
FF_CHUNK = 256
VMEM_LIMIT = 56 * 1024 * 1024


def _params(n_axes):
    return pltpu.CompilerParams(dimension_semantics=("arbitrary",) * n_axes,
                                vmem_limit_bytes=VMEM_LIMIT)


def _full(a):
    nd = a.ndim
    return pl.BlockSpec(a.shape, lambda *_: (0,) * nd)


def _dot(a, b):
    return jnp.dot(a, b, preferred_element_type=F32)


def _dot_nt(a, b):
    return lax.dot_general(a, b, (((1,), (1,)), ((), ())), preferred_element_type=F32)


def _rms(x, g):
    return x * lax.rsqrt(jnp.mean(x * x, axis=-1, keepdims=True) + EPS) * g


def _gelu(x):
    c = np.float32(np.sqrt(2.0 / np.pi))
    return x * (0.5 * (1.0 + jnp.tanh(c * (x + 0.044715 * (x * x * x)))))


def _sigmoid(x):
    return 1.0 / (1.0 + jnp.exp(-x))


def _sigmoid_tanh(x):
    return 0.5 * jnp.tanh(0.5 * x) + 0.5


def _ffn_kernel(x_ref, g_ref, win_ref, wout_ref, o_ref, act_ref, *, d_ff):
    x = x_ref[...]
    h = _rms(x, g_ref[...]).astype(BF16)
    for c in range(d_ff // FF_CHUNK):
        lo = c * FF_CHUNK
        gate = _dot(h, win_ref[:, lo:lo + FF_CHUNK])
        up = _dot(h, win_ref[:, d_ff + lo:d_ff + lo + FF_CHUNK])
        act_ref[:, lo:lo + FF_CHUNK] = (gate * _sigmoid(gate) * up).astype(BF16)
    o_ref[...] = x + 0.5 * _dot(act_ref[...], wout_ref[...])


def _ffn(x, g, w_in, w_out):
    B, S, D = x.shape
    d_ff = w_out.shape[0]
    tm = TM_FFN
    row = pl.BlockSpec((None, tm, D), lambda b, t: (b, t, 0))
    return pl.pallas_call(
        functools.partial(_ffn_kernel, d_ff=d_ff),
        grid=(B, S // tm),
        in_specs=[row, _full(g), _full(w_in), _full(w_out)],
        out_specs=row,
        out_shape=jax.ShapeDtypeStruct(x.shape, F32),
        scratch_shapes=[pltpu.VMEM((tm, d_ff), BF16)],
        compiler_params=_params(2),
        name="ffn",
    )(x, g, w_in, w_out)


def _conv_input(x_ref, xp_ref, xn_ref, g_ref, wx, cw_ref, cb_ref, hext_ref, ext_ref, tile, n_tiles, tm):
    g = g_ref[...]
    hext_ref[0:HALO, :] = _rms(xp_ref[...], g).astype(BF16)
    hext_ref[HALO:HALO + tm, :] = _rms(x_ref[...], g).astype(BF16)
    hext_ref[HALO + tm:2 * HALO + tm, :] = _rms(xn_ref[...], g).astype(BF16)
    z = _dot(hext_ref[...], wx)
    lpad = CONV_W // 2
    out = []
    for j in range(z.shape[1] // LANES):
        cols = slice(j * LANES, (j + 1) * LANES)
        ext_ref[j, 0:HALO, :] = jnp.where(tile > 0, z[0:HALO, cols], 0.0)
        ext_ref[j, HALO:HALO + tm, :] = z[HALO:HALO + tm, cols]
        ext_ref[j, HALO + tm:2 * HALO + tm, :] = jnp.where(tile < n_tiles - 1, z[HALO + tm:, cols], 0.0)
        acc = None
        for k in range(CONV_W):
            term = ext_ref[j, pl.ds(HALO - lpad + k, tm), :] * cw_ref[k:k + 1, cols]
            acc = term if acc is None else acc + term
        out.append(acc + cb_ref[:, cols])
    return jnp.concatenate(out, axis=1)


def _lru_inputs(xc, gw_ref, ba_ref, bi_ref, lam_ref):
    xcb = xc.astype(BF16)
    half = xcb.shape[1] // 2

    def gate(which, bias_ref):
        lo = _dot(xcb[:, :half], gw_ref[which, 0])
        hi = _dot(xcb[:, half:], gw_ref[which, 1])
        return _sigmoid_tanh(jnp.concatenate([lo, hi], axis=1) + bias_ref[...])

    r = gate(0, ba_ref)
    i = gate(1, bi_ref)
    z = -lam_ref[...]
    softplus = jnp.maximum(z, 0.0) + jnp.log1p(jnp.exp(-jnp.abs(z)))
    log_a = (-RG_C * r) * softplus
    a = jnp.exp(log_a)
    t = jnp.tanh(log_a)
    u = xc * i * jnp.sqrt(-2.0 * t / (1.0 - t))
    return a, u


def _group_scan(a, u, reverse):
    tm, c = a.shape
    a3 = a.reshape(tm // 8, 8, c)
    u3 = u.reshape(tm // 8, 8, c)
    row = lax.broadcasted_iota(jnp.int32, a3.shape, 1)
    for k in (1, 2, 4):
        shift, valid = (8 - k, row < 8 - k) if reverse else (k, row >= k)
        a_sh = jnp.where(valid, pltpu.roll(a3, shift, 1), 1.0)
        u_sh = jnp.where(valid, pltpu.roll(u3, shift, 1), 0.0)
        u3 = u3 + a3 * u_sh
        a3 = a3 * a_sh
    return a3.reshape(tm, c), u3.reshape(tm, c)


def _carry_scan(a_ref, l_ref, h_ref, carry_ref, tm, reverse):
    ng = tm // 8

    def body(j, c):
        g = ng - 1 - j if reverse else j
        r0 = pl.multiple_of(g * 8, 8)
        h = l_ref[pl.ds(r0, 8), :] + a_ref[pl.ds(r0, 8), :] * c
        h_ref[pl.ds(r0, 8), :] = h
        return h[0:1, :] if reverse else h[7:8, :]

    carry_ref[0:1, :] = lax.fori_loop(0, ng, body, carry_ref[0:1, :], unroll=4)


def _ev_bwd_kernel(x_ref, xp_ref, xn_ref, g_ref, wx_ref, cw_ref, cb_ref, gw_ref, ba_ref, bi_ref,
                   lam_ref, h_ref, hext_ref, ext_ref, a_ref, l_ref, carry_ref, *, tm, n_tiles):
    step = pl.program_id(1)
    tile = n_tiles - 1 - step

    @pl.when(step == 0)
    def _():
        carry_ref[...] = jnp.zeros_like(carry_ref)

    xc = _conv_input(x_ref, xp_ref, xn_ref, g_ref, wx_ref[...], cw_ref, cb_ref, hext_ref, ext_ref,
                     tile, n_tiles, tm)
    a, u = _lru_inputs(xc, gw_ref, ba_ref, bi_ref, lam_ref)
    a_ref[...], l_ref[...] = _group_scan(a, u, reverse=True)
    _carry_scan(a_ref, l_ref, h_ref, carry_ref, tm, reverse=True)


def _ev_main_kernel(x_ref, xp_ref, xn_ref, hb_ref, g_ref, win_ref, cw_ref, cb_ref, gw_ref, ba_ref,
                    bi_ref, lam_ref, lng_ref, lnb_ref, ws_ref, bs_ref, wout_ref, o_ref,
                    hext_ref, ext_ref, a_ref, l_ref, h_ref, carry_ref, cat_ref, *, tm, n_tiles, mix):
    tile = pl.program_id(1)

    @pl.when(tile == 0)
    def _():
        carry_ref[...] = jnp.zeros_like(carry_ref)

    xc = _conv_input(x_ref, xp_ref, xn_ref, g_ref, win_ref[:, 2 * mix:3 * mix], cw_ref, cb_ref,
                     hext_ref, ext_ref, tile, n_tiles, tm)
    a, u = _lru_inputs(xc, gw_ref, ba_ref, bi_ref, lam_ref)
    a_ref[...], l_ref[...] = _group_scan(a, u, reverse=False)
    _carry_scan(a_ref, l_ref, h_ref, carry_ref, tm, reverse=False)

    h = hext_ref[HALO:HALO + tm, :]
    z_g = _dot(h, win_ref[:, 3 * mix:4 * mix])
    cat_ref[:, mix:2 * mix] = ((h_ref[...] + hb_ref[...]) * _gelu(z_g)).astype(BF16)

    u_act = _gelu(_dot(h, win_ref[:, 0:mix]))
    v_act = _gelu(_dot(h, win_ref[:, mix:2 * mix]))
    mu = jnp.mean(v_act, axis=-1, keepdims=True)
    var = jnp.mean(jnp.square(v_act - mu), axis=-1, keepdims=True)
    v = ((v_act - mu) * lax.rsqrt(var + EPS) * lng_ref[...] + lnb_ref[...]).astype(BF16)
    gch = mix // GM_GROUPS
    for c in range(tm // GM_CHUNK):
        rows = slice(c * GM_CHUNK, (c + 1) * GM_CHUNK)
        for gi in range(GM_GROUPS):
            cols = slice(gi * gch, (gi + 1) * gch)
            s = _dot(ws_ref[gi], v[rows, cols]) + bs_ref[gi]
            cat_ref[rows, cols] = (u_act[rows, cols] * s).astype(BF16)

    o_ref[...] = x_ref[...] + _dot(cat_ref[...], wout_ref[...])


def _block_diag_tiles(w):
    nb, bs, _ = w.shape
    per = MXU_N // bs
    tiles = []
    for h in range(nb // per):
        t = jnp.zeros((MXU_N, MXU_N), w.dtype)
        for j in range(per):
            t = lax.dynamic_update_slice(t, w[h * per + j], (j * bs, j * bs))
        tiles.append(t)
    return jnp.stack(tiles)


def _even_mixer(x, g, w_in, w_out, ln_g, ln_b, w_s, b_s, conv_w, conv_b, w_a, b_a, w_i, b_i, lam):
    B, S, D = x.shape
    mix = w_in.shape[1] // 4
    tm = TM_MIX
    n_tiles = S // tm
    hpt = tm // HALO
    n_halo = S // HALO

    gw = [jnp.stack([_block_diag_tiles(w_a[d]), _block_diag_tiles(w_i[d])]).astype(BF16)
          for d in range(2)]
    row2 = lambda v: v.reshape(1, -1)
    bs_b = jnp.broadcast_to(b_s[:, :, None], (GM_GROUPS, GM_CHUNK, mix // GM_GROUPS))

    def specs(tile_of):
        return [
            pl.BlockSpec((None, tm, D), lambda b, t: (b, tile_of(t), 0)),
            pl.BlockSpec((None, HALO, D), lambda b, t: (b, jnp.maximum(tile_of(t) * hpt - 1, 0), 0)),
            pl.BlockSpec((None, HALO, D),
                         lambda b, t: (b, jnp.minimum((tile_of(t) + 1) * hpt, n_halo - 1), 0)),
        ]

    conv_scratch = [pltpu.VMEM((tm + 2 * HALO, D), BF16),
                    pltpu.VMEM((mix // LANES, tm + 2 * HALO, LANES), F32)]
    rev = lambda t: n_tiles - 1 - t
    wx = w_in[:, 2 * mix:3 * mix]
    bwd_args = (g, wx, conv_w, row2(conv_b), gw[1], row2(b_a[1]), row2(b_i[1]), row2(lam[1]))
    h_bwd = pl.pallas_call(
        functools.partial(_ev_bwd_kernel, tm=tm, n_tiles=n_tiles),
        grid=(B, n_tiles),
        in_specs=specs(rev) + [_full(a) for a in bwd_args],
        out_specs=pl.BlockSpec((None, tm, mix), lambda b, t: (b, rev(t), 0)),
        out_shape=jax.ShapeDtypeStruct((B, S, mix), F32),
        scratch_shapes=conv_scratch + [pltpu.VMEM((tm, mix), F32), pltpu.VMEM((tm, mix), F32),
                                       pltpu.VMEM((8, mix), F32)],
        compiler_params=_params(2),
        name="ev_bwd",
    )(x, x, x, *bwd_args)

    fwd = lambda t: t
    main_args = (g, w_in, conv_w, row2(conv_b), gw[0], row2(b_a[0]), row2(b_i[0]), row2(lam[0]),
                 row2(ln_g), row2(ln_b), w_s, bs_b, w_out)
    return pl.pallas_call(
        functools.partial(_ev_main_kernel, tm=tm, n_tiles=n_tiles, mix=mix),
        grid=(B, n_tiles),
        in_specs=specs(fwd) + [pl.BlockSpec((None, tm, mix), lambda b, t: (b, t, 0))]
        + [_full(a) for a in main_args],
        out_specs=pl.BlockSpec((None, tm, D), lambda b, t: (b, t, 0)),
        out_shape=jax.ShapeDtypeStruct(x.shape, F32),
        scratch_shapes=conv_scratch + [pltpu.VMEM((tm, mix), F32), pltpu.VMEM((tm, mix), F32),
                                       pltpu.VMEM((tm, mix), F32), pltpu.VMEM((8, mix), F32),
                                       pltpu.VMEM((tm, 2 * mix), BF16)],
        compiler_params=_params(2),
        name="ev_main",
    )(x, x, x, h_bwd, *main_args)


def _od_proj_kernel(x_ref, g_ref, w_ref, gsum_ref, qg_ref, kg_ref, cos_ref, sa_ref, sb_ref,
                    q1_ref, k1_ref, v1_ref, q4_ref, k4_ref, v4_ref, q16_ref, k16_ref, v16_ref,
                    nat_ref, d4_ref, *, d, tm):
    h = _rms(x_ref[...], g_ref[...]).astype(BF16)
    cos, sa, sb = cos_ref[...], sa_ref[...], sb_ref[...]
    half = ROT_DIM // 2

    def emit(hp, blk, o1_ref, o4_ref, o16_ref):
        o1_ref[hp] = blk.astype(BF16)
        nat_ref[hp] = blk
        for r1 in range(4):
            a = nat_ref[hp, pl.ds(r1, tm // 4, stride=4), :]
            o4_ref[hp, :, r1 * LANES:(r1 + 1) * LANES] = a.astype(BF16)
            d4_ref[4 * hp + r1] = a
            for r2 in range(4):
                b = d4_ref[4 * hp + r1, pl.ds(r2, tm // 16, stride=4), :]
                r = 4 * r2 + r1
                o16_ref[hp, :, r * LANES:(r + 1) * LANES] = b.astype(BF16)

    def normed_rotated(col0, gain_ref, outs, scale):
        y = _dot(h, w_ref[:, col0:col0 + d])
        sq = (y * y).astype(BF16)
        ss = jnp.concatenate([_dot(sq[:, j:j + MXU_N], gsum_ref[...]) for j in range(0, d, MXU_N)],
                             axis=1)
        yn = y * lax.rsqrt(ss * (1.0 / ATT_HD) + EPS) * gain_ref[...]
        for hp in range(d // LANES):
            blk = yn[:, hp * LANES:(hp + 1) * LANES]
            rot = (blk * cos + pltpu.roll(blk, half, 1) * sa
                   + pltpu.roll(blk, LANES - half, 1) * sb)
            emit(hp, rot * scale, *outs)

    normed_rotated(0, qg_ref, (q1_ref, q4_ref, q16_ref), ATT_HD ** -0.5)
    normed_rotated(d, kg_ref, (k1_ref, k4_ref, k16_ref), 1.0)
    v = _dot(h, w_ref[:, 2 * d:3 * d])
    for hp in range(d // LANES):
        emit(hp, v[:, hp * LANES:(hp + 1) * LANES], v1_ref, v4_ref, v16_ref)


def _rope_tables(S):
    half = ROT_DIM // 2
    inv = jnp.power(jnp.float32(ROPE_THETA), -jnp.arange(half, dtype=F32) * (2.0 / ROT_DIM))
    ang = jnp.arange(S, dtype=F32)[:, None] * inv[None, :]
    lane = np.arange(LANES) % ATT_HD
    idx = lane % half
    cos = jnp.where(lane < ROT_DIM, jnp.cos(ang)[:, idx], 1.0)
    sin = jnp.sin(ang)[:, idx]
    sin_a = jnp.where((lane >= half) & (lane < ROT_DIM), sin, 0.0)
    sin_b = jnp.where(lane < half, -sin, 0.0)
    return cos, sin_a, sin_b


def _keep_lanes(x, keep):
    return pltpu.bitcast(pltpu.bitcast(x, jnp.uint32) & keep, BF16)


def _attn_scores(q, k, bias, keep0, keep1):
    return (_dot_nt(_keep_lanes(q, keep0), k) + bias, _dot_nt(_keep_lanes(q, keep1), k) + bias)


def _attn_max(s0, s1, old):
    m0 = jnp.max(s0, axis=1, keepdims=True)
    m1 = jnp.max(s1, axis=1, keepdims=True)
    if old is None:
        return m0, m1
    return jnp.maximum(old[0], m0), jnp.maximum(old[1], m1)


def _sub_rows(s, m):
    if m.shape[1] == 1:
        return s - m
    return jnp.concatenate([s[:, j:j + LANES] - m for j in range(0, s.shape[1], LANES)], axis=1)


def _attn_update(s0, s1, v, m0, m1, old, head0, keep0, keep1, ones0, ones1):
    p = jnp.concatenate([jnp.exp(_sub_rows(s0, m0)), jnp.exp(_sub_rows(s1, m1))], axis=1).astype(BF16)
    rhs = jnp.concatenate([jnp.concatenate([_keep_lanes(v, keep0), ones0], axis=1),
                           jnp.concatenate([_keep_lanes(v, keep1), ones1], axis=1)], axis=0)
    r = _dot(p, rhs)
    acc, l = r[:, :LANES], r[:, LANES:]
    if old is not None:
        alpha = jnp.where(head0, jnp.exp(old[0] - m0), jnp.exp(old[1] - m1))
        l = alpha * old[2] + l
        acc = alpha * old[3] + acc
    full = (Q_TILE, LANES)
    return jnp.broadcast_to(m0, full), jnp.broadcast_to(m1, full), l, acc


def _od_attn_kernel(bias_ref, q1_ref, k1_ref, v1_ref, q4_ref, k4_ref, v4_ref, q16_ref, k16_ref,
                    v16_ref, o_ref, m0_ref, m1_ref, l_ref, acc_ref, *, S):
    head0 = lax.broadcasted_iota(jnp.int32, (1, LANES), 1) < ATT_HD
    all_bits = jnp.uint32(0xFFFFFFFF)
    keep0 = jnp.where(head0, all_bits, jnp.uint32(0))
    keep1 = jnp.where(head0, jnp.uint32(0), all_bits)
    ones0 = jnp.broadcast_to(jnp.where(head0, 1.0, 0.0).astype(BF16), (K_WIN, LANES))
    ones1 = jnp.broadcast_to(jnp.where(head0, 0.0, 1.0).astype(BF16), (K_WIN, LANES))
    state_refs = (m0_ref, m1_ref, l_ref, acc_ref)

    def window(q0, n):
        k0 = pl.multiple_of(jnp.clip(q0 - ATT_HALF, 0, n - K_WIN), ATT_HALF)
        return k0, lax.div(q0 - k0, ATT_HALF)

    def read_state(rows):
        return tuple(ref[rows, :] for ref in state_refs)

    def super_tile(st, _):
        tasks = []
        for r in range(16):
            tasks.append(((q16_ref, k16_ref, v16_ref), st * Q_TILE, S // 16,
                          slice(r * LANES, (r + 1) * LANES), pl.ds(r, Q_TILE, stride=16), False, False))
        for r in range(4):
            for tq in range(4):
                tasks.append(((q4_ref, k4_ref, v4_ref), st * (SUPER // 4) + tq * Q_TILE, S // 4,
                              slice(r * LANES, (r + 1) * LANES),
                              pl.ds(tq * (4 * Q_TILE) + r, Q_TILE, stride=4), True, False))
        for tq in range(16):
            tasks.append(((q1_ref, k1_ref, v1_ref), st * SUPER + tq * Q_TILE, S, slice(None),
                          pl.ds(tq * Q_TILE, Q_TILE), True, True))

        def stage_scores(task):
            (q_ref, k_ref, v_ref), q0, n, cols, _, _, _ = task
            q0 = pl.multiple_of(q0, Q_TILE)
            k0, bidx = window(q0, n)
            s0, s1 = _attn_scores(q_ref[pl.ds(q0, Q_TILE), cols], k_ref[pl.ds(k0, K_WIN), cols],
                                  bias_ref[bidx], keep0, keep1)
            return s0, s1, v_ref[pl.ds(k0, K_WIN), cols], q0

        def stage_max(task, carry):
            s0, s1, v, q0 = carry
            old = read_state(task[4]) if task[5] else None
            return s0, s1, v, q0, _attn_max(s0, s1, old), old

        def stage_update(task, carry):
            s0, s1, v, q0, (m0, m1), old = carry
            state = _attn_update(s0, s1, v, m0, m1, old, head0, keep0, keep1, ones0, ones1)
            if task[6]:
                _, _, l, acc = state
                o_ref[pl.ds(q0, Q_TILE), :] = (acc / l).astype(BF16)
            else:
                for ref, val in zip(state_refs, state):
                    ref[task[4], :] = val

        n = len(tasks)
        after_scores = {}
        for t in range(n + 1):
            if t < n:
                after_scores[t] = stage_scores(tasks[t])
            if t >= 1:
                stage_update(tasks[t - 1], stage_max(tasks[t - 1], after_scores.pop(t - 1)))
        return 0

    lax.fori_loop(0, S // SUPER, super_tile, 0)


def _band_bias():
    dji = np.arange(K_WIN)[None, :] - np.arange(Q_TILE)[:, None]
    offs = np.arange(3)[:, None, None] * ATT_HALF
    return np.where(np.abs(dji[None] - offs) <= ATT_HALF, 0.0, NEG_INF).astype(np.float32)


def _odd_mixer(x, g, w_in, q_g, k_g):
    B, S, D = x.shape
    tm = TM_MIX
    n_hp = D // LANES
    n_heads = D // ATT_HD
    assert S % SUPER == 0 and S // 16 >= K_WIN and tm % 256 == 0

    blk = jnp.kron(jnp.eye(MXU_N // ATT_HD, dtype=F32), jnp.ones((ATT_HD, ATT_HD), F32)).astype(BF16)
    tiled = lambda v: jnp.tile(v, n_heads).reshape(1, D)
    tables = _rope_tables(S)
    tab_spec = pl.BlockSpec((tm, LANES), lambda b, t: (t, 0))
    view_out_specs, view_shapes, view_in_specs = [], [], []
    for _, dil in DIL_PATTERNS:
        rows, cols = S // dil, dil * LANES
        view_out_specs += [pl.BlockSpec((None, n_hp, tm // dil, cols), lambda b, t: (b, 0, t, 0))] * 3
        view_shapes += [jax.ShapeDtypeStruct((B, n_hp, rows, cols), BF16)] * 3
        view_in_specs += [pl.BlockSpec((None, None, rows, cols), lambda b, h: (b, h, 0, 0))] * 3
    args = (g, w_in, blk, tiled(q_g), tiled(k_g))
    views = pl.pallas_call(
        functools.partial(_od_proj_kernel, d=D, tm=tm),
        grid=(B, S // tm),
        in_specs=[pl.BlockSpec((None, tm, D), lambda b, t: (b, t, 0))] + [_full(a) for a in args]
        + [tab_spec] * 3,
        out_specs=view_out_specs,
        out_shape=view_shapes,
        scratch_shapes=[pltpu.VMEM((n_hp, tm, LANES), F32), pltpu.VMEM((4 * n_hp, tm // 4, LANES), F32)],
        compiler_params=_params(2),
        name="od_proj",
    )(x, *args, *tables)

    bias = jnp.asarray(_band_bias())
    return pl.pallas_call(
        functools.partial(_od_attn_kernel, S=S),
        grid=(B, n_hp),
        in_specs=[_full(bias)] + view_in_specs,
        out_specs=pl.BlockSpec((None, S, LANES), lambda b, h: (b, 0, h)),
        out_shape=jax.ShapeDtypeStruct((B, S, D), BF16),
        scratch_shapes=[pltpu.VMEM((SUPER, LANES), F32)] * 4,
        compiler_params=_params(2),
        name="od_attn",
    )(bias, *views)


def _head_rms(y, gain):
    heads = []
    for j in range(0, y.shape[1], XA_HD):
        yh = y[:, j:j + XA_HD]
        heads.append(yh * lax.rsqrt(jnp.mean(yh * yh, axis=1, keepdims=True) + EPS))
    return jnp.concatenate(heads, axis=1) * gain


def _xa_kv_kernel(mem_ref, g_ref, w_ref, kg_ref, k_ref, v_ref, *, d):
    h = _rms(mem_ref[...], g_ref[...]).astype(BF16)
    k_ref[...] = _head_rms(_dot(h, w_ref[:, 0:d]), kg_ref[...]).astype(BF16)
    v_ref[...] = _dot(h, w_ref[:, d:2 * d]).astype(BF16)


def _xa_main_kernel(*refs, pre_proj, parts):
    if pre_proj:
        pre_ref, wpre_ref, *refs = refs
    x_ref, g_ref, wq_ref, qg_ref, k_ref, v_ref, wo_ref, o_ref, cat_ref = refs
    tm, d = x_ref.shape
    rows = [slice(i * (tm // parts), (i + 1) * (tm // parts)) for i in range(parts)]
    for r in rows:
        o_ref[r, :] = x_ref[r, :] + _dot(pre_ref[r, :], wpre_ref[...]) if pre_proj else x_ref[r, :]
    qs = [_head_rms(_dot(_rms(o_ref[r, :], g_ref[...]).astype(BF16), wq_ref[...]),
                    qg_ref[...]).astype(BF16) for r in rows]
    for j in range(0, d, XA_HD):
        ss = [_dot_nt(q[:, j:j + XA_HD], k_ref[:, j:j + XA_HD]) * (XA_HD ** -0.5) for q in qs]
        ps = [jnp.exp(s - jnp.max(s, axis=1, keepdims=True)) for s in ss]
        ps = [(p / jnp.sum(p, axis=1, keepdims=True)).astype(BF16) for p in ps]
        for r, p in zip(rows, ps):
            cat_ref[r, j:j + XA_HD] = _dot(p, v_ref[:, j:j + XA_HD]).astype(BF16)
    for r in rows:
        o_ref[r, :] = o_ref[r, :] + _dot(cat_ref[r, :], wo_ref[...])


def _cross_attn(x, mem, g_x, g_mem, w_q, w_kv, w_o, q_g, k_g, pre=None):
    B, S, D = x.shape
    M = mem.shape[1]
    tm = TM_XA
    n_heads = D // XA_HD
    tiled = lambda v: jnp.tile(v, n_heads).reshape(1, D)
    mem_spec = lambda dt: pl.BlockSpec((None, M, D), lambda b, *_: (b, 0, 0))
    kv_args = (g_mem, w_kv, tiled(k_g))
    k, v = pl.pallas_call(
        functools.partial(_xa_kv_kernel, d=D),
        grid=(B,),
        in_specs=[mem_spec(F32)] + [_full(a) for a in kv_args],
        out_specs=[mem_spec(BF16)] * 2,
        out_shape=[jax.ShapeDtypeStruct((B, M, D), BF16)] * 2,
        compiler_params=_params(1),
        name="xa_kv",
    )(mem, *kv_args)

    row = pl.BlockSpec((None, tm, D), lambda b, t: (b, t, 0))
    q_args = (g_x, w_q, tiled(q_g))
    pre_args = () if pre is None else pre
    pre_specs = [] if pre is None else [row, _full(pre[1])]
    return pl.pallas_call(
        functools.partial(_xa_main_kernel, pre_proj=pre is not None, parts=2),
        grid=(B, S // tm),
        in_specs=pre_specs + [row] + [_full(a) for a in q_args] + [mem_spec(BF16)] * 2 + [_full(w_o)],
        out_specs=row,
        out_shape=jax.ShapeDtypeStruct(x.shape, F32),
        scratch_shapes=[pltpu.VMEM((tm, D), BF16)],
        compiler_params=_params(2),
        name="xa_main",
    )(*pre_args, x, *q_args, k, v, w_o)


def _trunk(x, mem, P):
    depth = P['ffn1_norm'].shape[0]
    row = lambda v: v.reshape(1, -1)
    for l in range(depth):
        x = _ffn(x, row(P['ffn1_norm'][l]), P['ffn1_w_in'][l], P['ffn1_w_out'][l])
        g = row(P['mix_norm'][l])
        e = l // 2
        pre = None
        if l % 2 == 0:
            x = _even_mixer(x, g, P['ev_w_in'][e], P['ev_w_out'][e], P['gm_ln_g'][e], P['gm_ln_b'][e],
                            P['gm_w_s'][e], P['gm_b_s'][e], P['rg_conv_w'][e], P['rg_conv_b'][e],
                            P['rg_w_a'][e], P['rg_b_a'][e], P['rg_w_i'][e], P['rg_b_i'][e],
                            P['rg_lam'][e])
        else:
            pre = (_odd_mixer(x, g, P['od_w_in'][e], P['od_q_norm'][e], P['od_k_norm'][e]),
                   P['od_w_out'][e])
        x = _cross_attn(x, mem, row(P['xa_norm'][l]), row(P['xa_mem_norm'][l]), P['xa_w_q'][l],
                        P['xa_w_kv'][l], P['xa_w_o'][l], P['xa_q_norm'][l], P['xa_k_norm'][l], pre)
        x = _ffn(x, row(P['ffn2_norm'][l]), P['ffn2_w_in'][l], P['ffn2_w_out'][l])
    return x


_MXU_WEIGHTS = ('ffn1_w_in', 'ffn1_w_out', 'ev_w_in', 'ev_w_out', 'gm_w_s', 'od_w_in', 'od_w_out',
                'xa_w_q', 'xa_w_kv', 'xa_w_o', 'ffn2_w_in', 'ffn2_w_out')


def kernel(x_prompt, x_sample, mem_prompt, mem_sample, ffn1_norm, ffn1_w_in, ffn1_w_out, mix_norm, ev_w_in, ev_w_out, gm_ln_g, gm_ln_b, gm_w_s, gm_b_s, rg_conv_w, rg_conv_b, rg_w_a, rg_b_a, rg_w_i, rg_b_i, rg_lam, od_w_in, od_w_out, od_q_norm, od_k_norm, xa_norm, xa_mem_norm, xa_w_q, xa_w_kv, xa_w_o, xa_q_norm, xa_k_norm, ffn2_norm, ffn2_w_in, ffn2_w_out):
    P = dict(ffn1_norm=ffn1_norm, ffn1_w_in=ffn1_w_in, ffn1_w_out=ffn1_w_out, mix_norm=mix_norm,
             ev_w_in=ev_w_in, ev_w_out=ev_w_out, gm_ln_g=gm_ln_g, gm_ln_b=gm_ln_b,
             gm_w_s=gm_w_s, gm_b_s=gm_b_s, rg_conv_w=rg_conv_w, rg_conv_b=rg_conv_b,
             rg_w_a=rg_w_a, rg_b_a=rg_b_a, rg_w_i=rg_w_i, rg_b_i=rg_b_i, rg_lam=rg_lam,
             od_w_in=od_w_in, od_w_out=od_w_out, od_q_norm=od_q_norm, od_k_norm=od_k_norm,
             xa_norm=xa_norm, xa_mem_norm=xa_mem_norm, xa_w_q=xa_w_q, xa_w_kv=xa_w_kv,
             xa_w_o=xa_w_o, xa_q_norm=xa_q_norm, xa_k_norm=xa_k_norm,
             ffn2_norm=ffn2_norm, ffn2_w_in=ffn2_w_in, ffn2_w_out=ffn2_w_out)
    for name in _MXU_WEIGHTS:
        P[name] = P[name].astype(BF16)
    return (_trunk(x_prompt, mem_prompt, P), _trunk(x_sample, mem_sample, P))
```

```python
import functools

import numpy as np
import jax
import jax.numpy as jnp
from jax import lax
from jax.experimental import pallas as pl
from jax.experimental.pallas import tpu as pltpu

F32 = jnp.float32
BF16 = jnp.bfloat16

EPS = 1e-6
NEG_INF = -1e30
LOG2_E = 1.4426950408889634
RG_C = 8.0
GM_CHUNK = 128
GM_GROUPS = 4
CONV_W = 4
ATT_HD = 64
ROT_DIM = 16
ROPE_THETA = 500000.0
DIL_PATTERNS = ((128, 1), (512, 4), (2048, 16))
ATT_HALF = 64
XA_HD = 256

LANES = 128
MXU_N = 256
HALO = 16
Q_TILE = 128
K_WIN = Q_TILE + 2 * ATT_HALF
SUPER = Q_TILE * 16

TM_FFN = 512
TM_MIX = 512
TM_XA = 1024
FF_CHUNK = 256
VMEM_LIMIT = 56 * 1024 * 1024


def _params(n_axes):
    return pltpu.CompilerParams(dimension_semantics=("arbitrary",) * n_axes,
                                vmem_limit_bytes=VMEM_LIMIT)


def _full(a):
    nd = a.ndim
    return pl.BlockSpec(a.shape, lambda *_: (0,) * nd)


def _dot(a, b):
    return jnp.dot(a, b, preferred_element_type=F32)


def _dot_nt(a, b):
    return lax.dot_general(a, b, (((1,), (1,)), ((), ())), preferred_element_type=F32)


def _rms(x, g):
    return x * lax.rsqrt(jnp.mean(x * x, axis=-1, keepdims=True) + EPS) * g


def _gelu(x):
    c = np.float32(np.sqrt(2.0 / np.pi))
    return x * (0.5 * (1.0 + jnp.tanh(c * (x + 0.044715 * (x * x * x)))))


def _sigmoid(x):
    return 1.0 / (1.0 + jnp.exp(-x))


def _sigmoid_tanh(x):
    return 0.5 * jnp.tanh(0.5 * x) + 0.5


def _ffn_kernel(x_ref, g_ref, win_ref, wout_ref, o_ref, act_ref, *, d_ff):
    x = x_ref[...]
    h = _rms(x, g_ref[...]).astype(BF16)
    for c in range(d_ff // FF_CHUNK):
        lo = c * FF_CHUNK
        gate = _dot(h, win_ref[:, lo:lo + FF_CHUNK])
        up = _dot(h, win_ref[:, d_ff + lo:d_ff + lo + FF_CHUNK])
        act_ref[:, lo:lo + FF_CHUNK] = (gate * _sigmoid(gate) * up).astype(BF16)
    o_ref[...] = x + 0.5 * _dot(act_ref[...], wout_ref[...])


def _ffn(x, g, w_in, w_out):
    B, S, D = x.shape
    d_ff = w_out.shape[0]
    tm = TM_FFN
    row = pl.BlockSpec((None, tm, D), lambda b, t: (b, t, 0))
    return pl.pallas_call(
        functools.partial(_ffn_kernel, d_ff=d_ff),
        grid=(B, S // tm),
        in_specs=[row, _full(g), _full(w_in), _full(w_out)],
        out_specs=row,
        out_shape=jax.ShapeDtypeStruct(x.shape, F32),
        scratch_shapes=[pltpu.VMEM((tm, d_ff), BF16)],
        compiler_params=_params(2),
        name="ffn",
    )(x, g, w_in, w_out)


def _conv_input(x_ref, xp_ref, xn_ref, g_ref, wx, cw_ref, cb_ref, hext_ref, ext_ref, tile, n_tiles, tm):
    g = g_ref[...]
    hext_ref[0:HALO, :] = _rms(xp_ref[...], g).astype(BF16)
    hext_ref[HALO:HALO + tm, :] = _rms(x_ref[...], g).astype(BF16)
    hext_ref[HALO + tm:2 * HALO + tm, :] = _rms(xn_ref[...], g).astype(BF16)
    z = _dot(hext_ref[...], wx)
    lpad = CONV_W // 2
    out = []
    for j in range(z.shape[1] // LANES):
        cols = slice(j * LANES, (j + 1) * LANES)
        ext_ref[j, 0:HALO, :] = jnp.where(tile > 0, z[0:HALO, cols], 0.0)
        ext_ref[j, HALO:HALO + tm, :] = z[HALO:HALO + tm, cols]
        ext_ref[j, HALO + tm:2 * HALO + tm, :] = jnp.where(tile < n_tiles - 1, z[HALO + tm:, cols], 0.0)
        acc = None
        for k in range(CONV_W):
            term = ext_ref[j, pl.ds(HALO - lpad + k, tm), :] * cw_ref[k:k + 1, cols]
            acc = term if acc is None else acc + term
        out.append(acc + cb_ref[:, cols])
    return jnp.concatenate(out, axis=1)


def _lru_inputs(xc, gw_ref, ba_ref, bi_ref, lam_ref):
    xcb = xc.astype(BF16)
    half = xcb.shape[1] // 2

    def gate(which, bias_ref):
        lo = _dot(xcb[:, :half], gw_ref[which, 0])
        hi = _dot(xcb[:, half:], gw_ref[which, 1])
        return _sigmoid_tanh(jnp.concatenate([lo, hi], axis=1) + bias_ref[...])

    r = gate(0, ba_ref)
    i = gate(1, bi_ref)
    z = -lam_ref[...]
    softplus = jnp.maximum(z, 0.0) + jnp.log1p(jnp.exp(-jnp.abs(z)))
    log_a = (-RG_C * r) * softplus
    a = jnp.exp(log_a)
    t = jnp.tanh(log_a)
    u = xc * i * jnp.sqrt(-2.0 * t / (1.0 - t))
    return a, u


def _group_scan(a, u, reverse):
    tm, c = a.shape
    a3 = a.reshape(tm // 8, 8, c)
    u3 = u.reshape(tm // 8, 8, c)
    row = lax.broadcasted_iota(jnp.int32, a3.shape, 1)
    for k in (1, 2, 4):
        shift, valid = (8 - k, row < 8 - k) if reverse else (k, row >= k)
        a_sh = jnp.where(valid, pltpu.roll(a3, shift, 1), 1.0)
        u_sh = jnp.where(valid, pltpu.roll(u3, shift, 1), 0.0)
        u3 = u3 + a3 * u_sh
        a3 = a3 * a_sh
    return a3.reshape(tm, c), u3.reshape(tm, c)


def _carry_scan(a_ref, l_ref, h_ref, carry_ref, tm, reverse):
    ng = tm // 8

    def body(j, c):
        g = ng - 1 - j if reverse else j
        r0 = pl.multiple_of(g * 8, 8)
        h = l_ref[pl.ds(r0, 8), :] + a_ref[pl.ds(r0, 8), :] * c
        h_ref[pl.ds(r0, 8), :] = h
        return h[0:1, :] if reverse else h[7:8, :]

    carry_ref[0:1, :] = lax.fori_loop(0, ng, body, carry_ref[0:1, :], unroll=4)


def _ev_bwd_kernel(x_ref, xp_ref, xn_ref, g_ref, wx_ref, cw_ref, cb_ref, gw_ref, ba_ref, bi_ref,
                   lam_ref, h_ref, xc_ref, hext_ref, ext_ref, a_ref, l_ref, carry_ref, *, tm, n_tiles):
    step = pl.program_id(1)
    tile = n_tiles - 1 - step

    @pl.when(step == 0)
    def _():
        carry_ref[...] = jnp.zeros_like(carry_ref)

    xc = _conv_input(x_ref, xp_ref, xn_ref, g_ref, wx_ref[...], cw_ref, cb_ref, hext_ref, ext_ref,
                     tile, n_tiles, tm)
    xc_ref[...] = xc
    a, u = _lru_inputs(xc, gw_ref, ba_ref, bi_ref, lam_ref)
    a_ref[...], l_ref[...] = _group_scan(a, u, reverse=True)
    _carry_scan(a_ref, l_ref, h_ref, carry_ref, tm, reverse=True)


def _ev_main_kernel(x_ref, xc_ref, hb_ref, g_ref, win_ref, gw_ref, ba_ref, bi_ref, lam_ref, lng_ref,
                    lnb_ref, ws_ref, bs_ref, wout_ref, o_ref, a_ref, l_ref, h_ref, carry_ref, cat_ref,
                    *, tm, mix):
    @pl.when(pl.program_id(1) == 0)
    def _():
        carry_ref[...] = jnp.zeros_like(carry_ref)

    a, u = _lru_inputs(xc_ref[...], gw_ref, ba_ref, bi_ref, lam_ref)
    a_ref[...], l_ref[...] = _group_scan(a, u, reverse=False)
    _carry_scan(a_ref, l_ref, h_ref, carry_ref, tm, reverse=False)

    h = _rms(x_ref[...], g_ref[...]).astype(BF16)
    z_g = _dot(h, win_ref[:, 3 * mix:4 * mix])
    cat_ref[:, mix:2 * mix] = ((h_ref[...] + hb_ref[...]) * _gelu(z_g)).astype(BF16)

    u_act = _gelu(_dot(h, win_ref[:, 0:mix]))
    v_act = _gelu(_dot(h, win_ref[:, mix:2 * mix]))
    mu = jnp.mean(v_act, axis=-1, keepdims=True)
    var = jnp.mean(jnp.square(v_act - mu), axis=-1, keepdims=True)
    v = ((v_act - mu) * lax.rsqrt(var + EPS) * lng_ref[...] + lnb_ref[...]).astype(BF16)
    gch = mix // GM_GROUPS
    for c in range(tm // GM_CHUNK):
        rows = slice(c * GM_CHUNK, (c + 1) * GM_CHUNK)
        for gi in range(GM_GROUPS):
            cols = slice(gi * gch, (gi + 1) * gch)
            s = _dot(ws_ref[gi], v[rows, cols]) + bs_ref[gi]
            cat_ref[rows, cols] = (u_act[rows, cols] * s).astype(BF16)

    o_ref[...] = x_ref[...] + _dot(cat_ref[...], wout_ref[...])


def _block_diag_tiles(w):
    nb, bs, _ = w.shape
    per = MXU_N // bs
    tiles = []
    for h in range(nb // per):
        t = jnp.zeros((MXU_N, MXU_N), w.dtype)
        for j in range(per):
            t = lax.dynamic_update_slice(t, w[h * per + j], (j * bs, j * bs))
        tiles.append(t)
    return jnp.stack(tiles)


def _even_mixer(x, g, w_in, w_out, ln_g, ln_b, w_s, b_s, conv_w, conv_b, w_a, b_a, w_i, b_i, lam):
    B, S, D = x.shape
    mix = w_in.shape[1] // 4
    tm = TM_MIX
    n_tiles = S // tm
    hpt = tm // HALO
    n_halo = S // HALO

    gw = [jnp.stack([_block_diag_tiles(w_a[d]), _block_diag_tiles(w_i[d])]).astype(BF16)
          for d in range(2)]
    row2 = lambda v: v.reshape(1, -1)
    bs_b = jnp.broadcast_to(b_s[:, :, None], (GM_GROUPS, GM_CHUNK, mix // GM_GROUPS))

    rev = lambda t: n_tiles - 1 - t
    rev_specs = [
        pl.BlockSpec((None, tm, D), lambda b, t: (b, rev(t), 0)),
        pl.BlockSpec((None, HALO, D), lambda b, t: (b, jnp.maximum(rev(t) * hpt - 1, 0), 0)),
        pl.BlockSpec((None, HALO, D), lambda b, t: (b, jnp.minimum((rev(t) + 1) * hpt, n_halo - 1), 0)),
    ]
    rev_mix = pl.BlockSpec((None, tm, mix), lambda b, t: (b, rev(t), 0))
    mix_shape = jax.ShapeDtypeStruct((B, S, mix), F32)
    wx = w_in[:, 2 * mix:3 * mix]
    bwd_args = (g, wx, conv_w, row2(conv_b), gw[1], row2(b_a[1]), row2(b_i[1]), row2(lam[1]))
    h_bwd, xc = pl.pallas_call(
        functools.partial(_ev_bwd_kernel, tm=tm, n_tiles=n_tiles),
        grid=(B, n_tiles),
        in_specs=rev_specs + [_full(a) for a in bwd_args],
        out_specs=[rev_mix, rev_mix],
        out_shape=[mix_shape, mix_shape],
        scratch_shapes=[pltpu.VMEM((tm + 2 * HALO, D), BF16),
                        pltpu.VMEM((mix // LANES, tm + 2 * HALO, LANES), F32),
                        pltpu.VMEM((tm, mix), F32), pltpu.VMEM((tm, mix), F32),
                        pltpu.VMEM((8, mix), F32)],
        compiler_params=_params(2),
        name="ev_bwd",
    )(x, x, x, *bwd_args)

    fwd_mix = pl.BlockSpec((None, tm, mix), lambda b, t: (b, t, 0))
    fwd_row = pl.BlockSpec((None, tm, D), lambda b, t: (b, t, 0))
    main_args = (g, w_in, gw[0], row2(b_a[0]), row2(b_i[0]), row2(lam[0]),
                 row2(ln_g), row2(ln_b), w_s, bs_b, w_out)
    return pl.pallas_call(
        functools.partial(_ev_main_kernel, tm=tm, mix=mix),
        grid=(B, n_tiles),
        in_specs=[fwd_row, fwd_mix, fwd_mix] + [_full(a) for a in main_args],
        out_specs=fwd_row,
        out_shape=jax.ShapeDtypeStruct(x.shape, F32),
        scratch_shapes=[pltpu.VMEM((tm, mix), F32), pltpu.VMEM((tm, mix), F32),
                        pltpu.VMEM((tm, mix), F32), pltpu.VMEM((8, mix), F32),
                        pltpu.VMEM((tm, 2 * mix), BF16)],
        compiler_params=_params(2),
        name="ev_main",
    )(x, xc, h_bwd, *main_args)


def _od_proj_kernel(x_ref, g_ref, w_ref, gsum_ref, qg_ref, kg_ref, cos_ref, sa_ref, sb_ref,
                    q1_ref, k1_ref, v1_ref, q4_ref, k4_ref, v4_ref, q16_ref, k16_ref, v16_ref,
                    nat_ref, d4_ref, *, d, tm):
    h = _rms(x_ref[...], g_ref[...]).astype(BF16)
    cos, sa, sb = cos_ref[...], sa_ref[...], sb_ref[...]
    half = ROT_DIM // 2

    def emit(hp, blk, o1_ref, o4_ref, o16_ref):
        o1_ref[hp] = blk.astype(BF16)
        nat_ref[hp] = blk
        for r1 in range(4):
            a = nat_ref[hp, pl.ds(r1, tm // 4, stride=4), :]
            o4_ref[hp, :, r1 * LANES:(r1 + 1) * LANES] = a.astype(BF16)
            d4_ref[4 * hp + r1] = a
            for r2 in range(4):
                b = d4_ref[4 * hp + r1, pl.ds(r2, tm // 16, stride=4), :]
                r = 4 * r2 + r1
                o16_ref[hp, :, r * LANES:(r + 1) * LANES] = b.astype(BF16)

    def normed_rotated(col0, gain_ref, outs, scale):
        y = _dot(h, w_ref[:, col0:col0 + d])
        sq = (y * y).astype(BF16)
        ss = jnp.concatenate([_dot(sq[:, j:j + MXU_N], gsum_ref[...]) for j in range(0, d, MXU_N)],
                             axis=1)
        yn = y * lax.rsqrt(ss * (1.0 / ATT_HD) + EPS) * gain_ref[...]
        for hp in range(d // LANES):
            blk = yn[:, hp * LANES:(hp + 1) * LANES]
            rot = (blk * cos + pltpu.roll(blk, half, 1) * sa
                   + pltpu.roll(blk, LANES - half, 1) * sb)
            emit(hp, rot * scale, *outs)

    normed_rotated(0, qg_ref, (q1_ref, q4_ref, q16_ref), ATT_HD ** -0.5 * LOG2_E)
    normed_rotated(d, kg_ref, (k1_ref, k4_ref, k16_ref), 1.0)
    v = _dot(h, w_ref[:, 2 * d:3 * d])
    for hp in range(d // LANES):
        emit(hp, v[:, hp * LANES:(hp + 1) * LANES], v1_ref, v4_ref, v16_ref)


def _rope_tables(S):
    half = ROT_DIM // 2
    inv = jnp.power(jnp.float32(ROPE_THETA), -jnp.arange(half, dtype=F32) * (2.0 / ROT_DIM))
    ang = jnp.arange(S, dtype=F32)[:, None] * inv[None, :]
    lane = np.arange(LANES) % ATT_HD
    idx = lane % half
    cos = jnp.where(lane < ROT_DIM, jnp.cos(ang)[:, idx], 1.0)
    sin = jnp.sin(ang)[:, idx]
    sin_a = jnp.where((lane >= half) & (lane < ROT_DIM), sin, 0.0)
    sin_b = jnp.where(lane < half, -sin, 0.0)
    return cos, sin_a, sin_b


def _keep_lanes(x, keep):
    return pltpu.bitcast(pltpu.bitcast(x, jnp.uint32) & keep, BF16)


def _attn_scores(q, k, bias, keep0, keep1):
    return (_dot_nt(_keep_lanes(q, keep0), k) + bias, _dot_nt(_keep_lanes(q, keep1), k) + bias)


def _attn_max(s0, s1, old):
    m0 = jnp.max(s0, axis=1, keepdims=True)
    m1 = jnp.max(s1, axis=1, keepdims=True)
    if old is None:
        return m0, m1
    return jnp.maximum(old[0], m0), jnp.maximum(old[1], m1)


def _sub_rows(s, m):
    if m.shape[1] == 1:
        return s - m
    return jnp.concatenate([s[:, j:j + LANES] - m for j in range(0, s.shape[1], LANES)], axis=1)


def _attn_update(s0, s1, v, m0, m1, old, head0, keep0, keep1, ones0, ones1):
    p = jnp.concatenate([jnp.exp2(_sub_rows(s0, m0)), jnp.exp2(_sub_rows(s1, m1))], axis=1).astype(BF16)
    rhs = jnp.concatenate([jnp.concatenate([_keep_lanes(v, keep0), ones0], axis=1),
                           jnp.concatenate([_keep_lanes(v, keep1), ones1], axis=1)], axis=0)
    r = _dot(p, rhs)
    acc, l = r[:, :LANES], r[:, LANES:]
    if old is not None:
        alpha = jnp.exp2(jnp.where(head0, old[0], old[1]) - jnp.where(head0, m0, m1))
        l = alpha * old[2] + l
        acc = alpha * old[3] + acc
    full = (Q_TILE, LANES)
    return jnp.broadcast_to(m0, full), jnp.broadcast_to(m1, full), l, acc


def _od_attn_kernel(bias_ref, q1_ref, k1_ref, v1_ref, q4_ref, k4_ref, v4_ref, q16_ref, k16_ref,
                    v16_ref, o_ref, m0_ref, m1_ref, l_ref, acc_ref, *, S):
    head0 = lax.broadcasted_iota(jnp.int32, (1, LANES), 1) < ATT_HD
    all_bits = jnp.uint32(0xFFFFFFFF)
    keep0 = jnp.where(head0, all_bits, jnp.uint32(0))
    keep1 = jnp.where(head0, jnp.uint32(0), all_bits)
    ones0 = jnp.broadcast_to(jnp.where(head0, 1.0, 0.0).astype(BF16), (K_WIN, LANES))
    ones1 = jnp.broadcast_to(jnp.where(head0, 0.0, 1.0).astype(BF16), (K_WIN, LANES))
    state_refs = (m0_ref, m1_ref, l_ref, acc_ref)

    def window(q0, n):
        k0 = pl.multiple_of(jnp.clip(q0 - ATT_HALF, 0, n - K_WIN), ATT_HALF)
        return k0, lax.div(q0 - k0, ATT_HALF)

    def read_state(rows):
        return tuple(ref[rows, :] for ref in state_refs)

    def super_tile(st, _):
        tasks = []
        for r in range(16):
            tasks.append(((q16_ref, k16_ref, v16_ref), st * Q_TILE, S // 16,
                          slice(r * LANES, (r + 1) * LANES), pl.ds(r, Q_TILE, stride=16), False, False))
        for r in range(4):
            for tq in range(4):
                tasks.append(((q4_ref, k4_ref, v4_ref), st * (SUPER // 4) + tq * Q_TILE, S // 4,
                              slice(r * LANES, (r + 1) * LANES),
                              pl.ds(tq * (4 * Q_TILE) + r, Q_TILE, stride=4), True, False))
        for tq in range(16):
            tasks.append(((q1_ref, k1_ref, v1_ref), st * SUPER + tq * Q_TILE, S, slice(None),
                          pl.ds(tq * Q_TILE, Q_TILE), True, True))

        def stage_scores(task):
            (q_ref, k_ref, v_ref), q0, n, cols, _, _, _ = task
            q0 = pl.multiple_of(q0, Q_TILE)
            k0, bidx = window(q0, n)
            s0, s1 = _attn_scores(q_ref[pl.ds(q0, Q_TILE), cols], k_ref[pl.ds(k0, K_WIN), cols],
                                  bias_ref[bidx], keep0, keep1)
            return s0, s1, v_ref[pl.ds(k0, K_WIN), cols], q0

        def stage_max(task, carry):
            s0, s1, v, q0 = carry
            old = read_state(task[4]) if task[5] else None
            return s0, s1, v, q0, _attn_max(s0, s1, old), old

        def stage_update(task, carry):
            s0, s1, v, q0, (m0, m1), old = carry
            state = _attn_update(s0, s1, v, m0, m1, old, head0, keep0, keep1, ones0, ones1)
            if task[6]:
                _, _, l, acc = state
                o_ref[pl.ds(q0, Q_TILE), :] = (acc / l).astype(BF16)
            else:
                for ref, val in zip(state_refs, state):
                    ref[task[4], :] = val

        n = len(tasks)
        after_scores = {}
        for t in range(n + 1):
            if t < n:
                after_scores[t] = stage_scores(tasks[t])
            if t >= 1:
                stage_update(tasks[t - 1], stage_max(tasks[t - 1], after_scores.pop(t - 1)))
        return 0

    lax.fori_loop(0, S // SUPER, super_tile, 0)


def _band_bias():
    dji = np.arange(K_WIN)[None, :] - np.arange(Q_TILE)[:, None]
    offs = np.arange(3)[:, None, None] * ATT_HALF
    return np.where(np.abs(dji[None] - offs) <= ATT_HALF, 0.0, NEG_INF).astype(np.float32)


def _odd_mixer(x, g, w_in, q_g, k_g):
    B, S, D = x.shape
    tm = TM_MIX
    n_hp = D // LANES
    n_heads = D // ATT_HD
    assert S % SUPER == 0 and S // 16 >= K_WIN and tm % 256 == 0

    blk = jnp.kron(jnp.eye(MXU_N // ATT_HD, dtype=F32), jnp.ones((ATT_HD, ATT_HD), F32)).astype(BF16)
    tiled = lambda v: jnp.tile(v, n_heads).reshape(1, D)
    tables = _rope_tables(S)
    tab_spec = pl.BlockSpec((tm, LANES), lambda b, t: (t, 0))
    view_out_specs, view_shapes, view_in_specs = [], [], []
    for _, dil in DIL_PATTERNS:
        rows, cols = S // dil, dil * LANES
        view_out_specs += [pl.BlockSpec((None, n_hp, tm // dil, cols), lambda b, t: (b, 0, t, 0))] * 3
        view_shapes += [jax.ShapeDtypeStruct((B, n_hp, rows, cols), BF16)] * 3
        view_in_specs += [pl.BlockSpec((None, None, rows, cols), lambda b, h: (b, h, 0, 0))] * 3
    args = (g, w_in, blk, tiled(q_g), tiled(k_g))
    views = pl.pallas_call(
        functools.partial(_od_proj_kernel, d=D, tm=tm),
        grid=(B, S // tm),
        in_specs=[pl.BlockSpec((None, tm, D), lambda b, t: (b, t, 0))] + [_full(a) for a in args]
        + [tab_spec] * 3,
        out_specs=view_out_specs,
        out_shape=view_shapes,
        scratch_shapes=[pltpu.VMEM((n_hp, tm, LANES), F32), pltpu.VMEM((4 * n_hp, tm // 4, LANES), F32)],
        compiler_params=_params(2),
        name="od_proj",
    )(x, *args, *tables)

    bias = jnp.asarray(_band_bias())
    return pl.pallas_call(
        functools.partial(_od_attn_kernel, S=S),
        grid=(B, n_hp),
        in_specs=[_full(bias)] + view_in_specs,
        out_specs=pl.BlockSpec((None, S, LANES), lambda b, h: (b, 0, h)),
        out_shape=jax.ShapeDtypeStruct((B, S, D), BF16),
        scratch_shapes=[pltpu.VMEM((SUPER, LANES), F32)] * 4,
        compiler_params=_params(2),
        name="od_attn",
    )(bias, *views)


def _head_rms(y, gain):
    heads = []
    for j in range(0, y.shape[1], XA_HD):
        yh = y[:, j:j + XA_HD]
        heads.append(yh * lax.rsqrt(jnp.mean(yh * yh, axis=1, keepdims=True) + EPS))
    return jnp.concatenate(heads, axis=1) * gain


def _xa_kv_kernel(mem_ref, g_ref, w_ref, kg_ref, k_ref, v_ref, *, d):
    h = _rms(mem_ref[...], g_ref[...]).astype(BF16)
    k_ref[...] = _head_rms(_dot(h, w_ref[:, 0:d]), kg_ref[...]).astype(BF16)
    v_ref[...] = _dot(h, w_ref[:, d:2 * d]).astype(BF16)


def _xa_main_kernel(*refs, pre_proj, parts):
    if pre_proj:
        pre_ref, wpre_ref, *refs = refs
    x_ref, g_ref, wq_ref, qg_ref, k_ref, v_ref, wo_ref, o_ref, cat_ref = refs
    tm, d = x_ref.shape
    rows = [slice(i * (tm // parts), (i + 1) * (tm // parts)) for i in range(parts)]
    for r in rows:
        o_ref[r, :] = x_ref[r, :] + _dot(pre_ref[r, :], wpre_ref[...]) if pre_proj else x_ref[r, :]
    qs = [_head_rms(_dot(_rms(o_ref[r, :], g_ref[...]).astype(BF16), wq_ref[...]),
                    qg_ref[...]).astype(BF16) for r in rows]
    for j in range(0, d, XA_HD):
        ss = [_dot_nt(q[:, j:j + XA_HD], k_ref[:, j:j + XA_HD]) * (XA_HD ** -0.5) for q in qs]
        ps = [jnp.exp(s - jnp.max(s, axis=1, keepdims=True)) for s in ss]
        ps = [(p / jnp.sum(p, axis=1, keepdims=True)).astype(BF16) for p in ps]
        for r, p in zip(rows, ps):
            cat_ref[r, j:j + XA_HD] = _dot(p, v_ref[:, j:j + XA_HD]).astype(BF16)
    for r in rows:
        o_ref[r, :] = o_ref[r, :] + _dot(cat_ref[r, :], wo_ref[...])


def _cross_attn(x, mem, g_x, g_mem, w_q, w_kv, w_o, q_g, k_g, pre=None):
    B, S, D = x.shape
    M = mem.shape[1]
    tm = TM_XA
    n_heads = D // XA_HD
    tiled = lambda v: jnp.tile(v, n_heads).reshape(1, D)
    mem_spec = lambda dt: pl.BlockSpec((None, M, D), lambda b, *_: (b, 0, 0))
    kv_args = (g_mem, w_kv, tiled(k_g))
    k, v = pl.pallas_call(
        functools.partial(_xa_kv_kernel, d=D),
        grid=(B,),
        in_specs=[mem_spec(F32)] + [_full(a) for a in kv_args],
        out_specs=[mem_spec(BF16)] * 2,
        out_shape=[jax.ShapeDtypeStruct((B, M, D), BF16)] * 2,
        compiler_params=_params(1),
        name="xa_kv",
    )(mem, *kv_args)

    row = pl.BlockSpec((None, tm, D), lambda b, t: (b, t, 0))
    q_args = (g_x, w_q, tiled(q_g))
    pre_args = () if pre is None else pre
    pre_specs = [] if pre is None else [row, _full(pre[1])]
    return pl.pallas_call(
        functools.partial(_xa_main_kernel, pre_proj=pre is not None, parts=2),
        grid=(B, S // tm),
        in_specs=pre_specs + [row] + [_full(a) for a in q_args] + [mem_spec(BF16)] * 2 + [_full(w_o)],
        out_specs=row,
        out_shape=jax.ShapeDtypeStruct(x.shape, F32),
        scratch_shapes=[pltpu.VMEM((tm, D), BF16)],
        compiler_params=_params(2),
        name="xa_main",
    )(*pre_args, x, *q_args, k, v, w_o)


def _trunk(x, mem, P):
    depth = P['ffn1_norm'].shape[0]
    row = lambda v: v.reshape(1, -1)
    for l in range(depth):
        x = _ffn(x, row(P['ffn1_norm'][l]), P['ffn1_w_in'][l], P['ffn1_w_out'][l])
        g = row(P['mix_norm'][l])
        e = l // 2
        pre = None
        if l % 2 == 0:
            x = _even_mixer(x, g, P['ev_w_in'][e], P['ev_w_out'][e], P['gm_ln_g'][e], P['gm_ln_b'][e],
                            P['gm_w_s'][e], P['gm_b_s'][e], P['rg_conv_w'][e], P['rg_conv_b'][e],
                            P['rg_w_a'][e], P['rg_b_a'][e], P['rg_w_i'][e], P['rg_b_i'][e],
                            P['rg_lam'][e])
        else:
            pre = (_odd_mixer(x, g, P['od_w_in'][e], P['od_q_norm'][e], P['od_k_norm'][e]),
                   P['od_w_out'][e])
        x = _cross_attn(x, mem, row(P['xa_norm'][l]), row(P['xa_mem_norm'][l]), P['xa_w_q'][l],
                        P['xa_w_kv'][l], P['xa_w_o'][l], P['xa_q_norm'][l], P['xa_k_norm'][l], pre)
        x = _ffn(x, row(P['ffn2_norm'][l]), P['ffn2_w_in'][l], P['ffn2_w_out'][l])
    return x


_MXU_WEIGHTS = ('ffn1_w_in', 'ffn1_w_out', 'ev_w_in', 'ev_w_out', 'gm_w_s', 'od_w_in', 'od_w_out',
                'xa_w_q', 'xa_w_kv', 'xa_w_o', 'ffn2_w_in', 'ffn2_w_out')


def kernel(x_prompt, x_sample, mem_prompt, mem_sample, ffn1_norm, ffn1_w_in, ffn1_w_out, mix_norm, ev_w_in, ev_w_out, gm_ln_g, gm_ln_b, gm_w_s, gm_b_s, rg_conv_w, rg_conv_b, rg_w_a, rg_b_a, rg_w_i, rg_b_i, rg_lam, od_w_in, od_w_out, od_q_norm, od_k_norm, xa_norm, xa_mem_norm, xa_w_q, xa_w_kv, xa_w_o, xa_q_norm, xa_k_norm, ffn2_norm, ffn2_w_in, ffn2_w_out):
    P = dict(ffn1_norm=ffn1_norm, ffn1_w_in=ffn1_w_in, ffn1_w_out=ffn1_w_out, mix_norm=mix_norm,
             ev_w_in=ev_w_in, ev_w_out=ev_w_out, gm_ln_g=gm_ln_g, gm_ln_b=gm_ln_b,
             gm_w_s=gm_w_s, gm_b_s=gm_b_s, rg_conv_w=rg_conv_w, rg_conv_b=rg_conv_b,
             rg_w_a=rg_w_a, rg_b_a=rg_b_a, rg_w_i=rg_w_i, rg_b_i=rg_b_i, rg_lam=rg_lam,
             od_w_in=od_w_in, od_w_out=od_w_out, od_q_norm=od_q_norm, od_k_norm=od_k_norm,
             xa_norm=xa_norm, xa_mem_norm=xa_mem_norm, xa_w_q=xa_w_q, xa_w_kv=xa_w_kv,
             xa_w_o=xa_w_o, xa_q_norm=xa_q_norm, xa_k_norm=xa_k_norm,
             ffn2_norm=ffn2_norm, ffn2_w_in=ffn2_w_in, ffn2_w_out=ffn2_w_out)
    for name in _MXU_WEIGHTS:
        P[name] = P[name].astype(BF16)
    return (_trunk(x_prompt, mem_prompt, P), _trunk(x_sample, mem_sample, P))
```

```python
import functools

import numpy as np
import jax
import jax.numpy as jnp
from jax import lax
from jax.experimental import pallas as pl
from jax.experimental.pallas import tpu as pltpu

F32 = jnp.float32
BF16 = jnp.bfloat16

EPS = 1e-6
NEG_INF = -1e30
LOG2_E = 1.4426950408889634
RG_C = 8.0
GM_CHUNK = 128
GM_GROUPS = 4
CONV_W = 4
ATT_HD = 64
ROT_DIM = 16
ROPE_THETA = 500000.0
DIL_PATTERNS = ((128, 1), (512, 4), (2048, 16))
ATT_HALF = 64
XA_HD = 256

LANES = 128
MXU_N = 256
HALO = 16
Q_TILE = 128
K_WIN = Q_TILE + 2 * ATT_HALF
SUPER = Q_TILE * 16

TM_FFN = 1024
TM_MIX = 512
TM_XA = 1024
FF_CHUNK = 256
VMEM_LIMIT = 56 * 1024 * 1024


def _params(n_axes):
    return pltpu.CompilerParams(dimension_semantics=("arbitrary",) * n_axes,
                                vmem_limit_bytes=VMEM_LIMIT)


def _full(a, single_buffer=False):
    nd = a.ndim
    mode = dict(pipeline_mode=pl.Buffered(1)) if single_buffer else {}
    return pl.BlockSpec(a.shape, lambda *_: (0,) * nd, **mode)


def _layer(stack, layer, single_buffer=False):
    nd = stack.ndim - 1
    mode = dict(pipeline_mode=pl.Buffered(1)) if single_buffer else {}
    return pl.BlockSpec((None,) + stack.shape[1:], lambda *_: (layer,) + (0,) * nd, **mode)


def _dot(a, b):
    return jnp.dot(a, b, preferred_element_type=F32)


def _dot_nt(a, b):
    return lax.dot_general(a, b, (((1,), (1,)), ((), ())), preferred_element_type=F32)


def _rms(x, g):
    return x * lax.rsqrt(jnp.mean(x * x, axis=-1, keepdims=True) + EPS) * g


def _gelu(x):
    c = np.float32(np.sqrt(2.0 / np.pi))
    return x * (0.5 * (1.0 + jnp.tanh(c * (x + 0.044715 * (x * x * x)))))


def _sigmoid(x):
    return 1.0 / (1.0 + jnp.exp(-x))


def _sigmoid_tanh(x):
    return 0.5 * jnp.tanh(0.5 * x) + 0.5


def _ffn_kernel(x_ref, g_ref, win_ref, wout_ref, o_ref, act_ref, *, d_ff):
    x = x_ref[...]
    h = _rms(x, g_ref[...]).astype(BF16)
    for c in range(d_ff // FF_CHUNK):
        lo = c * FF_CHUNK
        gate = _dot(h, win_ref[:, lo:lo + FF_CHUNK])
        up = _dot(h, win_ref[:, d_ff + lo:d_ff + lo + FF_CHUNK])
        act_ref[:, lo:lo + FF_CHUNK] = (gate * _sigmoid(gate) * up).astype(BF16)
    o_ref[...] = x + 0.5 * _dot(act_ref[...], wout_ref[...])


def _ffn(x, g, w_in, w_out, layer):
    B, S, D = x.shape
    d_ff = w_out.shape[1]
    tm = TM_FFN
    row = pl.BlockSpec((None, tm, D), lambda b, t: (b, t, 0))
    return pl.pallas_call(
        functools.partial(_ffn_kernel, d_ff=d_ff),
        grid=(B, S // tm),
        in_specs=[row, _full(g), _layer(w_in, layer, True), _layer(w_out, layer, True)],
        out_specs=row,
        out_shape=jax.ShapeDtypeStruct(x.shape, F32),
        scratch_shapes=[pltpu.VMEM((tm, d_ff), BF16)],
        compiler_params=_params(2),
        name="ffn",
    )(x, g, w_in, w_out)


def _conv_input(x_ref, xp_ref, xn_ref, g_ref, wx, cw_ref, cb_ref, hext_ref, ext_ref, tile, n_tiles, tm):
    g = g_ref[...]
    hext_ref[0:HALO, :] = _rms(xp_ref[...], g).astype(BF16)
    hext_ref[HALO:HALO + tm, :] = _rms(x_ref[...], g).astype(BF16)
    hext_ref[HALO + tm:2 * HALO + tm, :] = _rms(xn_ref[...], g).astype(BF16)
    z = _dot(hext_ref[...], wx)
    lpad = CONV_W // 2
    out = []
    for j in range(z.shape[1] // LANES):
        cols = slice(j * LANES, (j + 1) * LANES)
        ext_ref[j, 0:HALO, :] = jnp.where(tile > 0, z[0:HALO, cols], 0.0)
        ext_ref[j, HALO:HALO + tm, :] = z[HALO:HALO + tm, cols]
        ext_ref[j, HALO + tm:2 * HALO + tm, :] = jnp.where(tile < n_tiles - 1, z[HALO + tm:, cols], 0.0)
        acc = None
        for k in range(CONV_W):
            term = ext_ref[j, pl.ds(HALO - lpad + k, tm), :] * cw_ref[k:k + 1, cols]
            acc = term if acc is None else acc + term
        out.append(acc + cb_ref[:, cols])
    return jnp.concatenate(out, axis=1)


def _lru_inputs(xc, gw_ref, ba_ref, bi_ref, lam_ref):
    xcb = xc.astype(BF16)
    half = xcb.shape[1] // 2

    def gate(which, bias_ref):
        lo = _dot(xcb[:, :half], gw_ref[which, 0])
        hi = _dot(xcb[:, half:], gw_ref[which, 1])
        return _sigmoid_tanh(jnp.concatenate([lo, hi], axis=1) + bias_ref[...])

    r = gate(0, ba_ref)
    i = gate(1, bi_ref)
    z = -lam_ref[...]
    softplus = jnp.maximum(z, 0.0) + jnp.log1p(jnp.exp(-jnp.abs(z)))
    log_a = (-RG_C * r) * softplus
    a = jnp.exp(log_a)
    t = jnp.tanh(log_a)
    y = -2.0 * t / (1.0 - t)
    u = xc * i * jnp.where(y > 0.0, y * lax.rsqrt(y), 0.0)
    return a, u


def _group_scan(a, u, reverse):
    tm, c = a.shape
    a3 = a.reshape(tm // 8, 8, c)
    u3 = u.reshape(tm // 8, 8, c)
    row = lax.broadcasted_iota(jnp.int32, a3.shape, 1)
    for k in (1, 2, 4):
        shift, valid = (8 - k, row < 8 - k) if reverse else (k, row >= k)
        a_sh = jnp.where(valid, pltpu.roll(a3, shift, 1), 1.0)
        u_sh = jnp.where(valid, pltpu.roll(u3, shift, 1), 0.0)
        u3 = u3 + a3 * u_sh
        a3 = a3 * a_sh
    return a3.reshape(tm, c), u3.reshape(tm, c)


def _carry_scan(a_ref, l_ref, h_ref, carry_ref, tm, reverse):
    ng = tm // 8

    def body(j, c):
        g = ng - 1 - j if reverse else j
        r0 = pl.multiple_of(g * 8, 8)
        h = l_ref[pl.ds(r0, 8), :] + a_ref[pl.ds(r0, 8), :] * c
        h_ref[pl.ds(r0, 8), :] = h
        return h[0:1, :] if reverse else h[7:8, :]

    carry_ref[0:1, :] = lax.fori_loop(0, ng, body, carry_ref[0:1, :], unroll=4)


def _ev_bwd_kernel(x_ref, xp_ref, xn_ref, g_ref, wx_ref, cw_ref, cb_ref, gw_ref, ba_ref, bi_ref,
                   lam_ref, h_ref, xc_ref, hext_ref, ext_ref, a_ref, l_ref, carry_ref, *, tm, n_tiles):
    step = pl.program_id(1)
    tile = n_tiles - 1 - step

    @pl.when(step == 0)
    def _():
        carry_ref[...] = jnp.zeros_like(carry_ref)

    xc = _conv_input(x_ref, xp_ref, xn_ref, g_ref, wx_ref[...], cw_ref, cb_ref, hext_ref, ext_ref,
                     tile, n_tiles, tm)
    xc_ref[...] = xc
    a, u = _lru_inputs(xc, gw_ref, ba_ref, bi_ref, lam_ref)
    a_ref[...], l_ref[...] = _group_scan(a, u, reverse=True)
    _carry_scan(a_ref, l_ref, h_ref, carry_ref, tm, reverse=True)


def _ev_main_kernel(x_ref, xc_ref, hb_ref, g_ref, win_ref, gw_ref, ba_ref, bi_ref, lam_ref, lng_ref,
                    lnb_ref, ws_ref, bs_ref, wout_ref, o_ref, a_ref, l_ref, h_ref, carry_ref, cat_ref,
                    *, tm, mix):
    @pl.when(pl.program_id(1) == 0)
    def _():
        carry_ref[...] = jnp.zeros_like(carry_ref)

    a, u = _lru_inputs(xc_ref[...], gw_ref, ba_ref, bi_ref, lam_ref)
    a_ref[...], l_ref[...] = _group_scan(a, u, reverse=False)
    _carry_scan(a_ref, l_ref, h_ref, carry_ref, tm, reverse=False)

    h = _rms(x_ref[...], g_ref[...]).astype(BF16)
    z_g = _dot(h, win_ref[:, 3 * mix:4 * mix])
    cat_ref[:, mix:2 * mix] = ((h_ref[...] + hb_ref[...]) * _gelu(z_g)).astype(BF16)

    u_act = _gelu(_dot(h, win_ref[:, 0:mix]))
    v_act = _gelu(_dot(h, win_ref[:, mix:2 * mix]))
    mu = jnp.mean(v_act, axis=-1, keepdims=True)
    var = jnp.mean(jnp.square(v_act - mu), axis=-1, keepdims=True)
    v = ((v_act - mu) * lax.rsqrt(var + EPS) * lng_ref[...] + lnb_ref[...]).astype(BF16)
    gch = mix // GM_GROUPS
    for c in range(tm // GM_CHUNK):
        rows = slice(c * GM_CHUNK, (c + 1) * GM_CHUNK)
        for gi in range(GM_GROUPS):
            cols = slice(gi * gch, (gi + 1) * gch)
            s = _dot(ws_ref[gi], v[rows, cols]) + bs_ref[gi]
            cat_ref[rows, cols] = (u_act[rows, cols] * s).astype(BF16)

    o_ref[...] = x_ref[...] + _dot(cat_ref[...], wout_ref[...])


def _block_diag_tiles(w):
    nb, bs, _ = w.shape
    per = MXU_N // bs
    tiles = []
    for h in range(nb // per):
        t = jnp.zeros((MXU_N, MXU_N), w.dtype)
        for j in range(per):
            t = lax.dynamic_update_slice(t, w[h * per + j], (j * bs, j * bs))
        tiles.append(t)
    return jnp.stack(tiles)


def _even_mixer(x, g, w_in, w_out, ln_g, ln_b, w_s, b_s, conv_w, conv_b, w_a, b_a, w_i, b_i, lam):
    B, S, D = x.shape
    mix = w_in.shape[1] // 4
    tm = TM_MIX
    n_tiles = S // tm
    hpt = tm // HALO
    n_halo = S // HALO

    gw = [jnp.stack([_block_diag_tiles(w_a[d]), _block_diag_tiles(w_i[d])]).astype(BF16)
          for d in range(2)]
    row2 = lambda v: v.reshape(1, -1)
    bs_b = jnp.broadcast_to(b_s[:, :, None], (GM_GROUPS, GM_CHUNK, mix // GM_GROUPS))

    rev = lambda t: n_tiles - 1 - t
    rev_specs = [
        pl.BlockSpec((None, tm, D), lambda b, t: (b, rev(t), 0)),
        pl.BlockSpec((None, HALO, D), lambda b, t: (b, jnp.maximum(rev(t) * hpt - 1, 0), 0)),
        pl.BlockSpec((None, HALO, D), lambda b, t: (b, jnp.minimum((rev(t) + 1) * hpt, n_halo - 1), 0)),
    ]
    rev_mix = pl.BlockSpec((None, tm, mix), lambda b, t: (b, rev(t), 0))
    mix_shape = jax.ShapeDtypeStruct((B, S, mix), F32)
    wx = w_in[:, 2 * mix:3 * mix]
    bwd_args = (g, wx, conv_w, row2(conv_b), gw[1], row2(b_a[1]), row2(b_i[1]), row2(lam[1]))
    h_bwd, xc = pl.pallas_call(
        functools.partial(_ev_bwd_kernel, tm=tm, n_tiles=n_tiles),
        grid=(B, n_tiles),
        in_specs=rev_specs + [_full(a) for a in bwd_args],
        out_specs=[rev_mix, rev_mix],
        out_shape=[mix_shape, mix_shape],
        scratch_shapes=[pltpu.VMEM((tm + 2 * HALO, D), BF16),
                        pltpu.VMEM((mix // LANES, tm + 2 * HALO, LANES), F32),
                        pltpu.VMEM((tm, mix), F32), pltpu.VMEM((tm, mix), F32),
                        pltpu.VMEM((8, mix), F32)],
        compiler_params=_params(2),
        name="ev_bwd",
    )(x, x, x, *bwd_args)

    fwd_mix = pl.BlockSpec((None, tm, mix), lambda b, t: (b, t, 0))
    fwd_row = pl.BlockSpec((None, tm, D), lambda b, t: (b, t, 0))
    main_args = (g, w_in, gw[0], row2(b_a[0]), row2(b_i[0]), row2(lam[0]),
                 row2(ln_g), row2(ln_b), w_s, bs_b, w_out)
    return pl.pallas_call(
        functools.partial(_ev_main_kernel, tm=tm, mix=mix),
        grid=(B, n_tiles),
        in_specs=[fwd_row, fwd_mix, fwd_mix] + [_full(a) for a in main_args],
        out_specs=fwd_row,
        out_shape=jax.ShapeDtypeStruct(x.shape, F32),
        scratch_shapes=[pltpu.VMEM((tm, mix), F32), pltpu.VMEM((tm, mix), F32),
                        pltpu.VMEM((tm, mix), F32), pltpu.VMEM((8, mix), F32),
                        pltpu.VMEM((tm, 2 * mix), BF16)],
        compiler_params=_params(2),
        name="ev_main",
    )(x, xc, h_bwd, *main_args)


def _od_proj_kernel(x_ref, g_ref, w_ref, gsum_ref, qg_ref, kg_ref, cos_ref, sa_ref, sb_ref,
                    q1_ref, k1_ref, v1_ref, q4_ref, k4_ref, v4_ref, q16_ref, k16_ref, v16_ref,
                    nat_ref, d4_ref, *, d, tm):
    h = _rms(x_ref[...], g_ref[...]).astype(BF16)
    cos, sa, sb = cos_ref[...], sa_ref[...], sb_ref[...]
    half = ROT_DIM // 2

    def emit(hp, blk, o1_ref, o4_ref, o16_ref):
        o1_ref[hp] = blk.astype(BF16)
        nat_ref[hp] = blk
        for r1 in range(4):
            a = nat_ref[hp, pl.ds(r1, tm // 4, stride=4), :]
            o4_ref[hp, :, r1 * LANES:(r1 + 1) * LANES] = a.astype(BF16)
            d4_ref[4 * hp + r1] = a
            for r2 in range(4):
                b = d4_ref[4 * hp + r1, pl.ds(r2, tm // 16, stride=4), :]
                r = 4 * r2 + r1
                o16_ref[hp, :, r * LANES:(r + 1) * LANES] = b.astype(BF16)

    def normed_rotated(col0, gain_ref, outs, scale):
        y = _dot(h, w_ref[:, col0:col0 + d])
        sq = (y * y).astype(BF16)
        ss = jnp.concatenate([_dot(sq[:, j:j + MXU_N], gsum_ref[...]) for j in range(0, d, MXU_N)],
                             axis=1)
        yn = y * lax.rsqrt(ss * (1.0 / ATT_HD) + EPS) * gain_ref[...]
        for hp in range(d // LANES):
            blk = yn[:, hp * LANES:(hp + 1) * LANES]
            rot = (blk * cos + pltpu.roll(blk, half, 1) * sa
                   + pltpu.roll(blk, LANES - half, 1) * sb)
            emit(hp, rot * scale, *outs)

    normed_rotated(0, qg_ref, (q1_ref, q4_ref, q16_ref), ATT_HD ** -0.5 * LOG2_E)
    normed_rotated(d, kg_ref, (k1_ref, k4_ref, k16_ref), 1.0)
    v = _dot(h, w_ref[:, 2 * d:3 * d])
    for hp in range(d // LANES):
        emit(hp, v[:, hp * LANES:(hp + 1) * LANES], v1_ref, v4_ref, v16_ref)


def _rope_tables(S):
    half = ROT_DIM // 2
    inv = jnp.power(jnp.float32(ROPE_THETA), -jnp.arange(half, dtype=F32) * (2.0 / ROT_DIM))
    ang = jnp.arange(S, dtype=F32)[:, None] * inv[None, :]
    lane = np.arange(LANES) % ATT_HD
    idx = lane % half
    cos = jnp.where(lane < ROT_DIM, jnp.cos(ang)[:, idx], 1.0)
    sin = jnp.sin(ang)[:, idx]
    sin_a = jnp.where((lane >= half) & (lane < ROT_DIM), sin, 0.0)
    sin_b = jnp.where(lane < half, -sin, 0.0)
    return cos, sin_a, sin_b


def _keep_lanes(x, keep):
    return x * keep


def _attn_scores(q, k, bias, keep0, keep1):
    return (_dot_nt(_keep_lanes(q, keep0), k) + bias, _dot_nt(_keep_lanes(q, keep1), k) + bias)


def _attn_max(s0, s1, old):
    m0 = jnp.max(s0, axis=1, keepdims=True)
    m1 = jnp.max(s1, axis=1, keepdims=True)
    if old is None:
        return m0, m1
    return jnp.maximum(old[0], m0), jnp.maximum(old[1], m1)


def _sub_rows(s, m):
    if m.shape[1] == 1:
        return s - m
    return jnp.concatenate([s[:, j:j + LANES] - m for j in range(0, s.shape[1], LANES)], axis=1)


def _attn_update(s0, s1, v, m0, m1, old, head0, keep0, keep1, ones0, ones1):
    p = jnp.concatenate([jnp.exp2(_sub_rows(s0, m0)), jnp.exp2(_sub_rows(s1, m1))], axis=1).astype(BF16)
    rhs = jnp.concatenate([jnp.concatenate([_keep_lanes(v, keep0), ones0], axis=1),
                           jnp.concatenate([_keep_lanes(v, keep1), ones1], axis=1)], axis=0)
    r = _dot(p, rhs)
    acc, l = r[:, :LANES], r[:, LANES:]
    if old is not None:
        alpha = jnp.exp2(jnp.where(head0, old[0], old[1]) - jnp.where(head0, m0, m1))
        l = alpha * old[2] + l
        acc = alpha * old[3] + acc
    full = (Q_TILE, LANES)
    return jnp.broadcast_to(m0, full), jnp.broadcast_to(m1, full), l, acc


def _od_attn_kernel(bias_ref, q1_ref, k1_ref, v1_ref, q4_ref, k4_ref, v4_ref, q16_ref, k16_ref,
                    v16_ref, o_ref, m0_ref, m1_ref, l_ref, acc_ref, *, S):
    head0 = lax.broadcasted_iota(jnp.int32, (1, LANES), 1) < ATT_HD
    keep0 = jnp.where(head0, 1.0, 0.0).astype(BF16)
    keep1 = jnp.where(head0, 0.0, 1.0).astype(BF16)
    ones0 = jnp.broadcast_to(keep0, (K_WIN, LANES))
    ones1 = jnp.broadcast_to(keep1, (K_WIN, LANES))
    state_refs = (m0_ref, m1_ref, l_ref, acc_ref)

    def window(q0, n):
        k0 = pl.multiple_of(jnp.clip(q0 - ATT_HALF, 0, n - K_WIN), ATT_HALF)
        return k0, lax.div(q0 - k0, ATT_HALF)

    def read_state(rows):
        return tuple(ref[rows, :] for ref in state_refs)

    def super_tile(st, _):
        tasks = []
        for r in range(16):
            tasks.append(((q16_ref, k16_ref, v16_ref), st * Q_TILE, S // 16,
                          slice(r * LANES, (r + 1) * LANES), pl.ds(r, Q_TILE, stride=16), False, False))
        for r in range(4):
            for tq in range(4):
                tasks.append(((q4_ref, k4_ref, v4_ref), st * (SUPER // 4) + tq * Q_TILE, S // 4,
                              slice(r * LANES, (r + 1) * LANES),
                              pl.ds(tq * (4 * Q_TILE) + r, Q_TILE, stride=4), True, False))
        for tq in range(16):
            tasks.append(((q1_ref, k1_ref, v1_ref), st * SUPER + tq * Q_TILE, S, slice(None),
                          pl.ds(tq * Q_TILE, Q_TILE), True, True))

        def stage_scores(task):
            (q_ref, k_ref, v_ref), q0, n, cols, _, _, _ = task
            q0 = pl.multiple_of(q0, Q_TILE)
            k0, bidx = window(q0, n)
            s0, s1 = _attn_scores(q_ref[pl.ds(q0, Q_TILE), cols], k_ref[pl.ds(k0, K_WIN), cols],
                                  bias_ref[bidx], keep0, keep1)
            return s0, s1, v_ref[pl.ds(k0, K_WIN), cols], q0

        def stage_max(task, carry):
            s0, s1, v, q0 = carry
            old = read_state(task[4]) if task[5] else None
            return s0, s1, v, q0, _attn_max(s0, s1, old), old

        def stage_update(task, carry):
            s0, s1, v, q0, (m0, m1), old = carry
            state = _attn_update(s0, s1, v, m0, m1, old, head0, keep0, keep1, ones0, ones1)
            if task[6]:
                _, _, l, acc = state
                o_ref[pl.ds(q0, Q_TILE), :] = (acc / l).astype(BF16)
            else:
                for ref, val in zip(state_refs, state):
                    ref[task[4], :] = val

        n = len(tasks)
        after_scores = {}
        for t in range(n + 1):
            if t < n:
                after_scores[t] = stage_scores(tasks[t])
            if t >= 1:
                stage_update(tasks[t - 1], stage_max(tasks[t - 1], after_scores.pop(t - 1)))
        return 0

    lax.fori_loop(0, S // SUPER, super_tile, 0)


def _band_bias():
    dji = np.arange(K_WIN)[None, :] - np.arange(Q_TILE)[:, None]
    offs = np.arange(3)[:, None, None] * ATT_HALF
    return np.where(np.abs(dji[None] - offs) <= ATT_HALF, 0.0, NEG_INF).astype(np.float32)


def _odd_mixer(x, g, w_in, q_g, k_g):
    B, S, D = x.shape
    tm = TM_MIX
    n_hp = D // LANES
    n_heads = D // ATT_HD
    assert S % SUPER == 0 and S // 16 >= K_WIN and tm % 256 == 0

    blk = jnp.kron(jnp.eye(MXU_N // ATT_HD, dtype=F32), jnp.ones((ATT_HD, ATT_HD), F32)).astype(BF16)
    tiled = lambda v: jnp.tile(v, n_heads).reshape(1, D)
    tables = _rope_tables(S)
    tab_spec = pl.BlockSpec((tm, LANES), lambda b, t: (t, 0))
    view_out_specs, view_shapes, view_in_specs = [], [], []
    for _, dil in DIL_PATTERNS:
        rows, cols = S // dil, dil * LANES
        view_out_specs += [pl.BlockSpec((None, n_hp, tm // dil, cols), lambda b, t: (b, 0, t, 0))] * 3
        view_shapes += [jax.ShapeDtypeStruct((B, n_hp, rows, cols), BF16)] * 3
        view_in_specs += [pl.BlockSpec((None, None, rows, cols), lambda b, h: (b, h, 0, 0))] * 3
    args = (g, w_in, blk, tiled(q_g), tiled(k_g))
    views = pl.pallas_call(
        functools.partial(_od_proj_kernel, d=D, tm=tm),
        grid=(B, S // tm),
        in_specs=[pl.BlockSpec((None, tm, D), lambda b, t: (b, t, 0))] + [_full(a) for a in args]
        + [tab_spec] * 3,
        out_specs=view_out_specs,
        out_shape=view_shapes,
        scratch_shapes=[pltpu.VMEM((n_hp, tm, LANES), F32), pltpu.VMEM((4 * n_hp, tm // 4, LANES), F32)],
        compiler_params=_params(2),
        name="od_proj",
    )(x, *args, *tables)

    bias = jnp.asarray(_band_bias())
    return pl.pallas_call(
        functools.partial(_od_attn_kernel, S=S),
        grid=(B, n_hp),
        in_specs=[_full(bias)] + view_in_specs,
        out_specs=pl.BlockSpec((None, S, LANES), lambda b, h: (b, 0, h)),
        out_shape=jax.ShapeDtypeStruct((B, S, D), BF16),
        scratch_shapes=[pltpu.VMEM((SUPER, LANES), F32)] * 4,
        compiler_params=_params(2),
        name="od_attn",
    )(bias, *views)


def _head_rms(y, gain):
    heads = []
    for j in range(0, y.shape[1], XA_HD):
        yh = y[:, j:j + XA_HD]
        heads.append(yh * lax.rsqrt(jnp.mean(yh * yh, axis=1, keepdims=True) + EPS))
    return jnp.concatenate(heads, axis=1) * gain


def _xa_kv_kernel(mem_ref, g_ref, w_ref, kg_ref, k_ref, v_ref, *, d):
    h = _rms(mem_ref[...], g_ref[...]).astype(BF16)
    k_ref[...] = _head_rms(_dot(h, w_ref[:, 0:d]), kg_ref[...]).astype(BF16)
    v_ref[...] = _dot(h, w_ref[:, d:2 * d]).astype(BF16)


def _xa_main_kernel(*refs, pre_proj, parts):
    if pre_proj:
        pre_ref, wpre_ref, *refs = refs
    x_ref, g_ref, wq_ref, qg_ref, k_ref, v_ref, wo_ref, o_ref, cat_ref = refs
    tm, d = x_ref.shape
    rows = [slice(i * (tm // parts), (i + 1) * (tm // parts)) for i in range(parts)]
    for r in rows:
        o_ref[r, :] = x_ref[r, :] + _dot(pre_ref[r, :], wpre_ref[...]) if pre_proj else x_ref[r, :]
    qs = [_head_rms(_dot(_rms(o_ref[r, :], g_ref[...]).astype(BF16), wq_ref[...]),
                    qg_ref[...]).astype(BF16) for r in rows]
    for j in range(0, d, XA_HD):
        ss = [_dot_nt(q[:, j:j + XA_HD], k_ref[:, j:j + XA_HD]) * (XA_HD ** -0.5) for q in qs]
        ps = [jnp.exp(s - jnp.max(s, axis=1, keepdims=True)) for s in ss]
        ps = [(p / jnp.sum(p, axis=1, keepdims=True)).astype(BF16) for p in ps]
        for r, p in zip(rows, ps):
            cat_ref[r, j:j + XA_HD] = _dot(p, v_ref[:, j:j + XA_HD]).astype(BF16)
    for r in rows:
        o_ref[r, :] = o_ref[r, :] + _dot(cat_ref[r, :], wo_ref[...])


def _cross_attn(x, mem, g_x, g_mem, w_q, w_kv, w_o, q_g, k_g, layer, pre=None):
    B, S, D = x.shape
    M = mem.shape[1]
    tm = TM_XA
    n_heads = D // XA_HD
    tiled = lambda v: jnp.tile(v, n_heads).reshape(1, D)
    mem_spec = lambda dt: pl.BlockSpec((None, M, D), lambda b, *_: (b, 0, 0))
    k_gain, q_gain = tiled(k_g), tiled(q_g)
    k, v = pl.pallas_call(
        functools.partial(_xa_kv_kernel, d=D),
        grid=(B,),
        in_specs=[mem_spec(F32), _full(g_mem), _layer(w_kv, layer), _full(k_gain)],
        out_specs=[mem_spec(BF16)] * 2,
        out_shape=[jax.ShapeDtypeStruct((B, M, D), BF16)] * 2,
        compiler_params=_params(1),
        name="xa_kv",
    )(mem, g_mem, w_kv, k_gain)

    row = pl.BlockSpec((None, tm, D), lambda b, t: (b, t, 0))
    pre_args = () if pre is None else pre
    pre_specs = [] if pre is None else [row, _full(pre[1])]
    return pl.pallas_call(
        functools.partial(_xa_main_kernel, pre_proj=pre is not None, parts=2),
        grid=(B, S // tm),
        in_specs=pre_specs + [row, _full(g_x), _layer(w_q, layer), _full(q_gain)]
        + [mem_spec(BF16)] * 2 + [_layer(w_o, layer)],
        out_specs=row,
        out_shape=jax.ShapeDtypeStruct(x.shape, F32),
        scratch_shapes=[pltpu.VMEM((tm, D), BF16)],
        compiler_params=_params(2),
        name="xa_main",
    )(*pre_args, x, g_x, w_q, q_gain, k, v, w_o)


def _trunk(x, mem, P):
    depth = P['ffn1_norm'].shape[0]
    row = lambda v: v.reshape(1, -1)
    for l in range(depth):
        x = _ffn(x, row(P['ffn1_norm'][l]), P['ffn1_w_in'], P['ffn1_w_out'], l)
        g = row(P['mix_norm'][l])
        e = l // 2
        pre = None
        if l % 2 == 0:
            x = _even_mixer(x, g, P['ev_w_in'][e], P['ev_w_out'][e], P['gm_ln_g'][e], P['gm_ln_b'][e],
                            P['gm_w_s'][e], P['gm_b_s'][e], P['rg_conv_w'][e], P['rg_conv_b'][e],
                            P['rg_w_a'][e], P['rg_b_a'][e], P['rg_w_i'][e], P['rg_b_i'][e],
                            P['rg_lam'][e])
        else:
            pre = (_odd_mixer(x, g, P['od_w_in'][e], P['od_q_norm'][e], P['od_k_norm'][e]),
                   P['od_w_out'][e])
        x = _cross_attn(x, mem, row(P['xa_norm'][l]), row(P['xa_mem_norm'][l]), P['xa_w_q'],
                        P['xa_w_kv'], P['xa_w_o'], P['xa_q_norm'][l], P['xa_k_norm'][l], l, pre)
        x = _ffn(x, row(P['ffn2_norm'][l]), P['ffn2_w_in'], P['ffn2_w_out'], l)
    return x


_MXU_WEIGHTS = ('ffn1_w_in', 'ffn1_w_out', 'ev_w_in', 'ev_w_out', 'gm_w_s', 'od_w_in', 'od_w_out',
                'xa_w_q', 'xa_w_kv', 'xa_w_o', 'ffn2_w_in', 'ffn2_w_out')


def kernel(x_prompt, x_sample, mem_prompt, mem_sample, ffn1_norm, ffn1_w_in, ffn1_w_out, mix_norm, ev_w_in, ev_w_out, gm_ln_g, gm_ln_b, gm_w_s, gm_b_s, rg_conv_w, rg_conv_b, rg_w_a, rg_b_a, rg_w_i, rg_b_i, rg_lam, od_w_in, od_w_out, od_q_norm, od_k_norm, xa_norm, xa_mem_norm, xa_w_q, xa_w_kv, xa_w_o, xa_q_norm, xa_k_norm, ffn2_norm, ffn2_w_in, ffn2_w_out):
    P = dict(ffn1_norm=ffn1_norm, ffn1_w_in=ffn1_w_in, ffn1_w_out=ffn1_w_out, mix_norm=mix_norm,
             ev_w_in=ev_w_in, ev_w_out=ev_w_out, gm_ln_g=gm_ln_g, gm_ln_b=gm_ln_b,
             gm_w_s=gm_w_s, gm_b_s=gm_b_s, rg_conv_w=rg_conv_w, rg_conv_b=rg_conv_b,
             rg_w_a=rg_w_a, rg_b_a=rg_b_a, rg_w_i=rg_w_i, rg_b_i=rg_b_i, rg_lam=rg_lam,
             od_w_in=od_w_in, od_w_out=od_w_out, od_q_norm=od_q_norm, od_k_norm=od_k_norm,
             xa_norm=xa_norm, xa_mem_norm=xa_mem_norm, xa_w_q=xa_w_q, xa_w_kv=xa_w_kv,
             xa_w_o=xa_w_o, xa_q_norm=xa_q_norm, xa_k_norm=xa_k_norm,
             ffn2_norm=ffn2_norm, ffn2_w_in=ffn2_w_in, ffn2_w_out=ffn2_w_out)
    for name in _MXU_WEIGHTS:
        P[name] = P[name].astype(BF16)
    return (_trunk(x_prompt, mem_prompt, P), _trunk(x_sample, mem_sample, P))
```
